```python
import jax, jax.numpy as jnp
from jax import lax
import numpy as np

D_MODEL = 2048
BATCH = 2
SEQ = 8192
DEPTH = 4

HEAD_DIM = 128
D_MIX = D_MODEL
N_DIFF_HEADS = D_MIX // (2 * HEAD_DIM)
N_SB_HEADS = D_MIX // (2 * HEAD_DIM)
D_DIFF = N_DIFF_HEADS * HEAD_DIM
D_SB = N_SB_HEADS * HEAD_DIM
DIFF_QK_DIM = HEAD_DIM // 2
ROPE_DIM = DIFF_QK_DIM // 4
ROPE_THETA = 500000.0
D_IN = 3 * D_DIFF + 3 * D_SB
D_FF = (11 * D_MODEL) // 4
CONV_WIDTH = 3
Q_BLOCK = 128
NORM_EPS = 1e-6
SUBLN_EPS = 1e-5

kernel_name = "hymba_diff_stickbreaking_convffn_trunk"


def _rmsnorm(x, g, eps=NORM_EPS):
    xf = x.astype(jnp.float32)
    xf = xf * lax.rsqrt(jnp.mean(xf * xf, axis=-1, keepdims=True) + eps)
    return xf.astype(x.dtype) * g


def _rope_tables(seq_len):
    inv_freq = ROPE_THETA ** (-jnp.arange(0, ROPE_DIM, 2, dtype=jnp.float32) / ROPE_DIM)
    pos = jnp.arange(seq_len, dtype=jnp.float32)
    ang = pos[:, None] * inv_freq[None, :]
    return jnp.cos(ang), jnp.sin(ang)


def _partial_rope(x, cos, sin):
    half = ROPE_DIM // 2
    c = cos.astype(x.dtype)
    s = sin.astype(x.dtype)
    x1 = x[..., :half]
    x2 = x[..., half:ROPE_DIM]
    return jnp.concatenate([x1 * c - x2 * s, x2 * c + x1 * s, x[..., ROPE_DIM:]], axis=-1)


def _attention_blocks(q1, q2, k1, k2, vd, lam, qs, ks, vs):
    seq_len = q1.shape[2]
    diff_scale = DIFF_QK_DIM ** -0.5
    sb_scale = HEAD_DIM ** -0.5
    q_local = jnp.arange(Q_BLOCK)
    outs_d, outs_s = [], []
    for i in range(seq_len // Q_BLOCK):
        q0 = i * Q_BLOCK
        kend = q0 + Q_BLOCK
        qpos = q0 + q_local
        kpos = jnp.arange(kend)
        mask_incl = kpos[None, :] <= qpos[:, None]
        mask_strict = kpos[None, :] < qpos[:, None]

        s1 = jnp.einsum('bhqd,bhkd->bhqk', q1[:, :, q0:kend], k1[:, :, :kend]).astype(jnp.float32) * diff_scale
        s2 = jnp.einsum('bhqd,bhkd->bhqk', q2[:, :, q0:kend], k2[:, :, :kend]).astype(jnp.float32) * diff_scale
        p1 = jax.nn.softmax(jnp.where(mask_incl, s1, -jnp.inf), axis=-1)
        p2 = jax.nn.softmax(jnp.where(mask_incl, s2, -jnp.inf), axis=-1)
        w_diff = (p1 - lam * p2).astype(vd.dtype)
        outs_d.append(jnp.einsum('bhqk,bhkd->bhqd', w_diff, vd[:, :, :kend]))

        z = jnp.einsum('bhqd,bhkd->bhqk', qs[:, :, q0:kend], ks[:, :, :kend]).astype(jnp.float32) * sb_scale
        log_beta = jax.nn.log_sigmoid(z)
        log_1m = jnp.where(mask_strict, jax.nn.log_sigmoid(-z), 0.0)
        suffix = lax.cumsum(log_1m, axis=3, reverse=True) - log_1m
        a_sb = jnp.where(mask_strict, jnp.exp(log_beta + suffix), 0.0).astype(vs.dtype)
        outs_s.append(jnp.einsum('bhqk,bhkd->bhqd', a_sb, vs[:, :, :kend]))
    return jnp.concatenate(outs_d, axis=2), jnp.concatenate(outs_s, axis=2)


def _causal_dwconv(u, w, b):
    seq_len = u.shape[1]
    up = jnp.pad(u, ((0, 0), (CONV_WIDTH - 1, 0), (0, 0)))
    out = b
    for k in range(CONV_WIDTH):
        out = out + w[k] * up[:, k:k + seq_len]
    return out


def setup_inputs(seed: int = 0) -> dict:
    key = jax.random.key(seed)
    ks = jax.random.split(key, 17)
    f32 = jnp.float32
    nrm = lambda k, shape, scale: jax.random.normal(k, shape, f32) * scale
    gain = lambda k, shape: 1.0 + 0.05 * jax.random.normal(k, shape, f32)
    return {
        "x": jax.random.normal(ks[0], (BATCH, SEQ, D_MODEL), f32),
        "attn_pre_norm": gain(ks[1], (DEPTH, D_MODEL)),
        "w_in": nrm(ks[2], (DEPTH, D_MODEL, D_IN), D_MODEL ** -0.5),
        "diff_lambda_q1": nrm(ks[3], (DEPTH, DIFF_QK_DIM), 0.1),
        "diff_lambda_k1": nrm(ks[4], (DEPTH, DIFF_QK_DIM), 0.1),
        "diff_lambda_q2": nrm(ks[5], (DEPTH, DIFF_QK_DIM), 0.1),
        "diff_lambda_k2": nrm(ks[6], (DEPTH, DIFF_QK_DIM), 0.1),
        "diff_subln": gain(ks[7], (DEPTH, HEAD_DIM)),
        "w_out": nrm(ks[8], (DEPTH, D_MIX, D_MODEL), D_MIX ** -0.5),
        "attn_post_norm": gain(ks[9], (DEPTH, D_MODEL)),
        "ffn_pre_norm": gain(ks[10], (DEPTH, D_MODEL)),
        "ffn_w_up": nrm(ks[11], (DEPTH, D_MODEL, 2 * D_FF), D_MODEL ** -0.5),
        "ffn_conv_w": nrm(ks[12], (DEPTH, CONV_WIDTH, 2 * D_FF), CONV_WIDTH ** -0.5),
        "ffn_conv_b": nrm(ks[13], (DEPTH, 2 * D_FF), 0.01),
        "ffn_w_down": nrm(ks[14], (DEPTH, D_FF, D_MODEL), D_FF ** -0.5),
        "ffn_post_norm": gain(ks[15], (DEPTH, D_MODEL)),
    }


def reference(x, attn_pre_norm, w_in, diff_lambda_q1, diff_lambda_k1, diff_lambda_q2,
              diff_lambda_k2, diff_subln, w_out, attn_post_norm, ffn_pre_norm, ffn_w_up,
              ffn_conv_w, ffn_conv_b, ffn_w_down, ffn_post_norm):
    bsz, seq_len, _ = x.shape
    cos, sin = _rope_tables(seq_len)
    splits = [D_DIFF, 2 * D_DIFF, 3 * D_DIFF, 3 * D_DIFF + D_SB, 3 * D_DIFF + 2 * D_SB]
    for l in range(DEPTH):
        h = _rmsnorm(x, attn_pre_norm[l])
        proj = h @ w_in[l]
        dq, dk, dv, sq, sk, sv = jnp.split(proj, splits, axis=-1)

        dq = dq.reshape(bsz, seq_len, N_DIFF_HEADS, 2, DIFF_QK_DIM).transpose(3, 0, 2, 1, 4)
        dk = dk.reshape(bsz, seq_len, N_DIFF_HEADS, 2, DIFF_QK_DIM).transpose(3, 0, 2, 1, 4)
        dq = _partial_rope(dq, cos, sin)
        dk = _partial_rope(dk, cos, sin)
        dv = dv.reshape(bsz, seq_len, N_DIFF_HEADS, HEAD_DIM).transpose(0, 2, 1, 3)
        lambda_init = 0.8 - 0.6 * float(np.exp(-0.3 * l))
        lam = (jnp.exp(jnp.sum(diff_lambda_q1[l].astype(jnp.float32) * diff_lambda_k1[l].astype(jnp.float32)))
               - jnp.exp(jnp.sum(diff_lambda_q2[l].astype(jnp.float32) * diff_lambda_k2[l].astype(jnp.float32)))
               + lambda_init)

        sq = sq.reshape(bsz, seq_len, N_SB_HEADS, HEAD_DIM).transpose(0, 2, 1, 3)
        sk = sk.reshape(bsz, seq_len, N_SB_HEADS, HEAD_DIM).transpose(0, 2, 1, 3)
        sv = sv.reshape(bsz, seq_len, N_SB_HEADS, HEAD_DIM).transpose(0, 2, 1, 3)

        o_d, o_s = _attention_blocks(dq[0], dq[1], dk[0], dk[1], dv, lam, sq, sk, sv)
        o_d = _rmsnorm(o_d, diff_subln[l], SUBLN_EPS) * (1.0 - lambda_init)
        o_d = o_d.transpose(0, 2, 1, 3).reshape(bsz, seq_len, D_DIFF)
        o_s = o_s.transpose(0, 2, 1, 3).reshape(bsz, seq_len, D_SB)
        mix = jnp.concatenate([o_d, o_s], axis=-1) @ w_out[l]
        x = x + _rmsnorm(mix, attn_post_norm[l])

        h = _rmsnorm(x, ffn_pre_norm[l])
        u = _causal_dwconv(h @ ffn_w_up[l], ffn_conv_w[l], ffn_conv_b[l])
        gate, val = jnp.split(u, 2, axis=-1)
        y = (jax.nn.gelu(gate, approximate=True) * val) @ ffn_w_down[l]
        x = x + _rmsnorm(y, ffn_post_norm[l])
    return x
```

```python
import functools
import math

import jax
import jax.numpy as jnp
import numpy as np
from jax import lax
from jax.experimental import pallas as pl
from jax.experimental.pallas import tpu as pltpu

HEAD_DIM = 128
DIFF_QK_DIM = HEAD_DIM // 2
ROPE_DIM = DIFF_QK_DIM // 4
ROPE_HALF = ROPE_DIM // 2
ROPE_THETA = 500000.0
NORM_EPS = 1e-6
SUBLN_EPS = 1e-5
CONV_WIDTH = 3
LANES = 128
BF16_SUBLANES = 16
MASK_VALUE = -1e30
VMEM_LIMIT_BYTES = 56 * 1024 * 1024

F32 = jnp.float32
BF16 = jnp.bfloat16


def _tile(n, pref):
    if n <= pref:
        return n
    for t in range(pref - pref % LANES, 0, -LANES):
        if n % t == 0:
            return t
    raise ValueError((n, pref))


def _rms_scale(x, eps):
    return x * lax.rsqrt(jnp.mean(x * x, axis=-1, keepdims=True) + eps)


def _params(*sem):
    return pltpu.CompilerParams(dimension_semantics=sem, vmem_limit_bytes=VMEM_LIMIT_BYTES)


def _in_proj_kernel(x_ref, g_ref, w_ref, cos_ref, sa_ref, sb_ref, o_ref, h_ref, *, n_rope_tiles):
    j = pl.program_id(1)

    @pl.when(j == 0)
    def _():
        h_ref[...] = (_rms_scale(x_ref[...], NORM_EPS) * g_ref[...]).astype(BF16)

    acc = jnp.dot(h_ref[...], w_ref[...], preferred_element_type=F32)
    tn = acc.shape[1]

    @pl.when(j < n_rope_tiles)
    def _():
        for c in range(tn // LANES):
            a = acc[:, c * LANES:(c + 1) * LANES]
            r = (a * cos_ref[...]
                 + pltpu.roll(a, LANES - ROPE_HALF, 1) * sa_ref[...]
                 + pltpu.roll(a, ROPE_HALF, 1) * sb_ref[...])
            o_ref[:, c * LANES:(c + 1) * LANES] = r.astype(BF16)

    @pl.when(j >= n_rope_tiles)
    def _():
        o_ref[...] = acc.astype(BF16)


def _in_proj(x, g, w, layer, tables, seq_len, n_rope_cols):
    m, d = x.shape
    n = w.shape[2]
    tm = _tile(seq_len, 1024)
    tn = _tile(n_rope_cols, 512)
    assert n % tn == 0
    tiles_per_seq = seq_len // tm
    cos_t, sa_t, sb_t = tables
    tab_spec = pl.BlockSpec((tm, LANES), lambda i, j: (i % tiles_per_seq, 0))
    return pl.pallas_call(
        functools.partial(_in_proj_kernel, n_rope_tiles=n_rope_cols // tn),
        grid=(m // tm, n // tn),
        in_specs=[
            pl.BlockSpec((tm, d), lambda i, j: (i, 0)),
            pl.BlockSpec((None, 1, d), lambda i, j: (layer, 0, 0)),
            pl.BlockSpec((None, d, tn), lambda i, j: (layer, 0, j)),
            tab_spec, tab_spec, tab_spec,
        ],
        out_specs=pl.BlockSpec((tm, tn), lambda i, j: (i, j)),
        out_shape=jax.ShapeDtypeStruct((m, n), BF16),
        scratch_shapes=[pltpu.VMEM((tm, d), BF16)],
        compiler_params=_params("arbitrary", "arbitrary"),
        name="in_proj",
    )(x, g, w, cos_t, sa_t, sb_t)


def _diff_attn_kernel(lam_ref, g_ref, q_ref, k_ref, v_ref, o_ref, *, t, lambda_init):
    qi = pl.program_id(2)
    q = q_ref[...] * (DIFF_QK_DIM ** -0.5)
    lane = lax.broadcasted_iota(jnp.int32, (t, HEAD_DIM), 1)
    zero = jnp.zeros_like(q)
    qq = jnp.concatenate([jnp.where(lane < DIFF_QK_DIM, q, zero),
                          jnp.where(lane >= DIFF_QK_DIM, q, zero)], axis=0)
    row = lax.broadcasted_iota(jnp.int32, (2 * t, t), 0)
    col = lax.broadcasted_iota(jnp.int32, (2 * t, t), 1)
    causal = col <= jnp.where(row >= t, row - t, row)

    def block(j, carry, masked):
        m, l, acc = carry
        start = pl.multiple_of(j * t, t)
        k = k_ref[pl.ds(start, t), :]
        v = v_ref[pl.ds(start, t), :]
        s = lax.dot_general(qq, k, (((1,), (1,)), ((), ())), preferred_element_type=F32)
        if masked:
            s = jnp.where(causal, s, MASK_VALUE)
        m_new = jnp.maximum(m, jnp.max(s, axis=1, keepdims=True))
        alpha = jnp.exp(m - m_new)
        p = jnp.exp(s - m_new)
        l = alpha * l + jnp.sum(p, axis=1, keepdims=True)
        acc = alpha * acc + jnp.dot(p.astype(BF16), v, preferred_element_type=F32)
        return m_new, l, acc

    init = (jnp.full((2 * t, 1), MASK_VALUE, F32), jnp.zeros((2 * t, 1), F32),
            jnp.zeros((2 * t, HEAD_DIM), F32))
    carry = block(qi, init, True)
    m, l, acc = lax.fori_loop(0, qi, lambda j, c: block(j, c, False), carry)

    lp = lam_ref[...]
    lam = (jnp.exp(jnp.sum(lp[0:1] * lp[1:2], axis=1, keepdims=True))
           - jnp.exp(jnp.sum(lp[2:3] * lp[3:4], axis=1, keepdims=True)) + lambda_init)
    o = acc[:t] / l[:t] - lam * (acc[t:] / l[t:])
    o = _rms_scale(o, SUBLN_EPS) * g_ref[...] * (1.0 - lambda_init)
    o_ref[...] = o.astype(BF16)


def _diff_attn(proj, lam_params, subln, layer, lambda_init, batch, seq_len, n_heads):
    t = _tile(seq_len, 256)
    nq = seq_len // t
    return pl.pallas_call(
        functools.partial(_diff_attn_kernel, t=t, lambda_init=lambda_init),
        grid=(batch, n_heads, nq),
        in_specs=[
            pl.BlockSpec((None, 4, DIFF_QK_DIM), lambda b, h, i: (layer, 0, 0)),
            pl.BlockSpec((None, 1, HEAD_DIM), lambda b, h, i: (layer, 0, 0)),
            pl.BlockSpec((t, HEAD_DIM), lambda b, h, i: (b * nq + i, h)),
            pl.BlockSpec((seq_len, HEAD_DIM), lambda b, h, i: (b, n_heads + h)),
            pl.BlockSpec((seq_len, HEAD_DIM), lambda b, h, i: (b, 2 * n_heads + h)),
        ],
        out_specs=pl.BlockSpec((t, HEAD_DIM), lambda b, h, i: (b * nq + i, h)),
        out_shape=jax.ShapeDtypeStruct((batch * seq_len, n_heads * HEAD_DIM), BF16),
        compiler_params=_params("arbitrary", "arbitrary", "arbitrary"),
        name="diff_attn",
    )(lam_params, subln, proj, proj, proj)


def _sb_attn_kernel(q_ref, k_ref, v_ref, o_ref, *, t):
    qi = pl.program_id(2)
    q = q_ref[...]
    scale = HEAD_DIM ** -0.5
    row = lax.broadcasted_iota(jnp.int32, (t, t), 0)
    col = lax.broadcasted_iota(jnp.int32, (t, t), 1)
    strict = col < row
    tj = lax.broadcasted_iota(jnp.int32, (2 * t, t + LANES), 0)
    ts = lax.broadcasted_iota(jnp.int32, (2 * t, t + LANES), 1)
    tj = jnp.where(tj >= t, tj - t, tj)
    tri = jnp.where((tj > ts) | (ts >= t), 1.0, 0.0).astype(BF16)

    def block(j, carry, masked):
        c, acc = carry
        start = pl.multiple_of(j * t, t)
        k = k_ref[pl.ds(start, t), :]
        v = v_ref[pl.ds(start, t), :]
        z = lax.dot_general(q, k, (((1,), (1,)), ((), ())), preferred_element_type=F32) * scale
        sp = jnp.log1p(jnp.exp(-jnp.abs(z)))
        log_beta = jnp.minimum(z, 0.0) - sp
        log_1m = -jnp.maximum(z, 0.0) - sp
        if masked:
            log_1m = jnp.where(strict, log_1m, 0.0)
        hi = log_1m.astype(BF16)
        lo = (log_1m - hi.astype(F32)).astype(BF16)
        ext = jnp.dot(jnp.concatenate([hi, lo], axis=1), tri, preferred_element_type=F32)
        suffix = ext[:, :t]
        rowsum = ext[:, t:]
        a = jnp.exp(log_beta + suffix + jnp.tile(c, (1, t // LANES)))
        if masked:
            a = jnp.where(strict, a, 0.0)
        acc = acc + jnp.dot(a.astype(BF16), v, preferred_element_type=F32)
        return c + rowsum, acc

    init = (jnp.zeros((t, LANES), F32), jnp.zeros((t, HEAD_DIM), F32))
    carry = block(qi, init, True)
    _, acc = lax.fori_loop(0, qi, lambda n, cr: block(qi - 1 - n, cr, False), carry)
    o_ref[...] = acc.astype(BF16)


def _sb_attn(proj, batch, seq_len, n_heads, col0):
    t = _tile(seq_len, 256)
    nq = seq_len // t
    return pl.pallas_call(
        functools.partial(_sb_attn_kernel, t=t),
        grid=(batch, n_heads, nq),
        in_specs=[
            pl.BlockSpec((t, HEAD_DIM), lambda b, h, i: (b * nq + i, col0 + h)),
            pl.BlockSpec((seq_len, HEAD_DIM), lambda b, h, i: (b, col0 + n_heads + h)),
            pl.BlockSpec((seq_len, HEAD_DIM), lambda b, h, i: (b, col0 + 2 * n_heads + h)),
        ],
        out_specs=pl.BlockSpec((t, HEAD_DIM), lambda b, h, i: (b * nq + i, h)),
        out_shape=jax.ShapeDtypeStruct((batch * seq_len, n_heads * HEAD_DIM), BF16),
        compiler_params=_params("arbitrary", "arbitrary", "arbitrary"),
        name="sb_attn",
    )(proj, proj, proj)


def _out_proj_kernel(ad_ref, as_ref, w_ref, g_ref, x_ref, o_ref):
    dd = ad_ref.shape[1]
    mix = (jnp.dot(ad_ref[...], w_ref[:dd, :], preferred_element_type=F32)
           + jnp.dot(as_ref[...], w_ref[dd:, :], preferred_element_type=F32))
    o_ref[...] = x_ref[...] + _rms_scale(mix, NORM_EPS) * g_ref[...]


def _out_proj(o_d, o_s, w, g, x, layer):
    m, d = x.shape
    dd, ds = o_d.shape[1], o_s.shape[1]
    tm = _tile(m, 512)
    return pl.pallas_call(
        _out_proj_kernel,
        grid=(m // tm,),
        in_specs=[
            pl.BlockSpec((tm, dd), lambda i: (i, 0)),
            pl.BlockSpec((tm, ds), lambda i: (i, 0)),
            pl.BlockSpec((None, dd + ds, d), lambda i: (layer, 0, 0)),
            pl.BlockSpec((None, 1, d), lambda i: (layer, 0, 0)),
            pl.BlockSpec((tm, d), lambda i: (i, 0)),
        ],
        out_specs=pl.BlockSpec((tm, d), lambda i: (i, 0)),
        out_shape=jax.ShapeDtypeStruct((m, d), F32),
        compiler_params=_params("arbitrary"),
        name="out_proj",
    )(o_d, o_s, w, g, x)


def _ffn_kernel(x_ref, halo_ref, gpre_ref, wg_ref, wv_ref, cwg_ref, cwv_ref, cbg_ref, cbv_ref,
                wd_ref, gpost_ref, o_ref, h_ref, acc_ref, *, tiles_per_seq):
    i = pl.program_id(0)
    f = pl.program_id(1)
    tm = x_ref.shape[0]
    halo = halo_ref.shape[0]

    @pl.when(f == 0)
    def _():
        g = gpre_ref[...]
        h_ref[halo:, :] = (_rms_scale(x_ref[...], NORM_EPS) * g).astype(BF16)
        hh = _rms_scale(halo_ref[...], NORM_EPS) * g
        hh = jnp.where(i % tiles_per_seq == 0, 0.0, hh)
        h_ref[:halo, :] = hh.astype(BF16)
        acc_ref[...] = jnp.zeros_like(acc_ref)

    h = h_ref[...]

    def conv(w_ref, cw_ref, cb_ref):
        u = jnp.dot(h, w_ref[...], preferred_element_type=F32)
        cw = cw_ref[...]
        out = cb_ref[...]
        for tap in range(CONV_WIDTH):
            off = halo - (CONV_WIDTH - 1) + tap
            out = out + cw[tap:tap + 1, :] * u[off:off + tm, :]
        return out

    gate = conv(wg_ref, cwg_ref, cbg_ref)
    val = conv(wv_ref, cwv_ref, cbv_ref)
    gelu = 0.5 * gate * (1.0 + jnp.tanh(math.sqrt(2.0 / math.pi) * (gate + 0.044715 * (gate * gate * gate))))
    acc_ref[...] += jnp.dot((gelu * val).astype(BF16), wd_ref[...], preferred_element_type=F32)

    @pl.when(f == pl.num_programs(1) - 1)
    def _():
        o_ref[...] = x_ref[...] + _rms_scale(acc_ref[...], NORM_EPS) * gpost_ref[...]


def _ffn(x, gpre, w_up, conv_w, conv_b, w_down, gpost, layer, seq_len):
    m, d = x.shape
    d_ff = w_down.shape[1]
    tm = _tile(seq_len, 512)
    tf = _tile(d_ff, 512)
    nf = d_ff // tf
    halo = BF16_SUBLANES
    tiles_per_seq = seq_len // tm
    vec = lambda off: pl.BlockSpec((None, 1, tf), lambda i, f: (layer, 0, off + f))
    return pl.pallas_call(
        functools.partial(_ffn_kernel, tiles_per_seq=tiles_per_seq),
        grid=(m // tm, nf),
        in_specs=[
            pl.BlockSpec((tm, d), lambda i, f: (i, 0)),
            pl.BlockSpec((halo, d), lambda i, f: (jnp.maximum(i * (tm // halo) - 1, 0), 0)),
            pl.BlockSpec((None, 1, d), lambda i, f: (layer, 0, 0)),
            pl.BlockSpec((None, d, tf), lambda i, f: (layer, 0, f)),
            pl.BlockSpec((None, d, tf), lambda i, f: (layer, 0, nf + f)),
            pl.BlockSpec((None, CONV_WIDTH, tf), lambda i, f: (layer, 0, f)),
            pl.BlockSpec((None, CONV_WIDTH, tf), lambda i, f: (layer, 0, nf + f)),
            vec(0), vec(nf),
            pl.BlockSpec((None, tf, d), lambda i, f: (layer, f, 0)),
            pl.BlockSpec((None, 1, d), lambda i, f: (layer, 0, 0)),
        ],
        out_specs=pl.BlockSpec((tm, d), lambda i, f: (i, 0)),
        out_shape=jax.ShapeDtypeStruct((m, d), F32),
        scratch_shapes=[pltpu.VMEM((halo + tm, d), BF16), pltpu.VMEM((tm, d), F32)],
        compiler_params=_params("arbitrary", "arbitrary"),
        name="conv_ffn",
    )(x, x, gpre, w_up, w_up, conv_w, conv_w, conv_b, conv_b, w_down, gpost)


def _rope_tables(seq_len):
    inv_freq = ROPE_THETA ** (-jnp.arange(0, ROPE_DIM, 2, dtype=F32) / ROPE_DIM)
    ang = jnp.arange(seq_len, dtype=F32)[:, None] * inv_freq[None, :]
    cos, sin = jnp.cos(ang), jnp.sin(ang)
    rest = DIFF_QK_DIM - ROPE_DIM
    ones = jnp.ones((seq_len, rest), F32)
    z_half = jnp.zeros((seq_len, ROPE_HALF), F32)
    z_rest = jnp.zeros((seq_len, rest), F32)
    reps = HEAD_DIM // DIFF_QK_DIM
    cos_t = jnp.tile(jnp.concatenate([cos, cos, ones], axis=1), (1, reps))
    sa_t = jnp.tile(jnp.concatenate([-sin, z_half, z_rest], axis=1), (1, reps))
    sb_t = jnp.tile(jnp.concatenate([z_half, sin, z_rest], axis=1), (1, reps))
    return cos_t, sa_t, sb_t


def kernel(x, attn_pre_norm, w_in, diff_lambda_q1, diff_lambda_k1, diff_lambda_q2, diff_lambda_k2,
           diff_subln, w_out, attn_post_norm, ffn_pre_norm, ffn_w_up, ffn_conv_w, ffn_conv_b,
           ffn_w_down, ffn_post_norm):
    batch, seq_len, d_model = x.shape
    depth = w_in.shape[0]
    n_heads = d_model // (2 * HEAD_DIM)
    d_group = n_heads * HEAD_DIM
    assert w_in.shape[2] == 6 * d_group

    xf = x.reshape(batch * seq_len, d_model)
    w_in_b, w_out_b = w_in.astype(BF16), w_out.astype(BF16)
    w_up_b, w_down_b = ffn_w_up.astype(BF16), ffn_w_down.astype(BF16)
    row3 = lambda a: a.reshape(depth, 1, a.shape[-1])
    lam_params = jnp.stack([diff_lambda_q1, diff_lambda_k1, diff_lambda_q2, diff_lambda_k2], axis=1)
    tables = _rope_tables(seq_len)

    for l in range(depth):
        lambda_init = 0.8 - 0.6 * float(np.exp(-0.3 * l))
        proj = _in_proj(xf, row3(attn_pre_norm), w_in_b, l, tables, seq_len, 2 * d_group)
        o_d = _diff_attn(proj, lam_params, row3(diff_subln), l, lambda_init, batch, seq_len, n_heads)
        o_s = _sb_attn(proj, batch, seq_len, n_heads, 3 * n_heads)
        xf = _out_proj(o_d, o_s, w_out_b, row3(attn_post_norm), xf, l)
        xf = _ffn(xf, row3(ffn_pre_norm), w_up_b, ffn_conv_w, row3(ffn_conv_b), w_down_b,
                  row3(ffn_post_norm), l, seq_len)
    return xf.reshape(batch, seq_len, d_model)
```

```python
import functools
import math

import jax
import jax.numpy as jnp
import numpy as np
from jax import lax
from jax.experimental import pallas as pl
from jax.experimental.pallas import tpu as pltpu

HEAD_DIM = 128
DIFF_QK_DIM = HEAD_DIM // 2
ROPE_DIM = DIFF_QK_DIM // 4
ROPE_HALF = ROPE_DIM // 2
ROPE_THETA = 500000.0
NORM_EPS = 1e-6
SUBLN_EPS = 1e-5
CONV_WIDTH = 3
LANES = 128
F32_SUBLANES = 8
BF16_SUBLANES = 16
ATTN_HEADS_PER_STEP = 2
MASK_VALUE = -1e30
SOFTMAX_M_INIT = -1e29
LOG2E = math.log2(math.e)
VMEM_LIMIT_BYTES = 56 * 1024 * 1024

F32 = jnp.float32
BF16 = jnp.bfloat16


def _tile(n, pref):
    if n <= pref:
        return n
    for t in range(pref - pref % LANES, 0, -LANES):
        if n % t == 0:
            return t
    raise ValueError((n, pref))


def _rms_scale(x, eps):
    return x * lax.rsqrt(jnp.mean(x * x, axis=-1, keepdims=True) + eps)


def _params(*sem):
    return pltpu.CompilerParams(dimension_semantics=sem, vmem_limit_bytes=VMEM_LIMIT_BYTES)


def _in_proj_kernel(x_ref, g_ref, w_ref, cos_ref, sa_ref, sb_ref, o_ref, h_ref, *, n_rope_tiles):
    j = pl.program_id(1)

    @pl.when(j == 0)
    def _():
        h_ref[...] = (_rms_scale(x_ref[...], NORM_EPS) * g_ref[...]).astype(BF16)

    acc = jnp.dot(h_ref[...], w_ref[...], preferred_element_type=F32)
    tn = acc.shape[1]

    @pl.when(j < n_rope_tiles)
    def _():
        for c in range(tn // LANES):
            a = acc[:, c * LANES:(c + 1) * LANES]
            r = (a * cos_ref[...]
                 + pltpu.roll(a, LANES - ROPE_HALF, 1) * sa_ref[...]
                 + pltpu.roll(a, ROPE_HALF, 1) * sb_ref[...])
            o_ref[:, c * LANES:(c + 1) * LANES] = r.astype(BF16)

    @pl.when(j >= n_rope_tiles)
    def _():
        o_ref[...] = acc.astype(BF16)


def _in_proj(x, g, w, layer, tables, seq_len, n_rope_cols):
    m, d = x.shape
    n = w.shape[2]
    tm = _tile(seq_len, 1024)
    tn = _tile(n_rope_cols, 512)
    assert n % tn == 0
    tiles_per_seq = seq_len // tm
    cos_t, sa_t, sb_t = tables
    tab_spec = pl.BlockSpec((tm, LANES), lambda i, j: (i % tiles_per_seq, 0))
    return pl.pallas_call(
        functools.partial(_in_proj_kernel, n_rope_tiles=n_rope_cols // tn),
        grid=(m // tm, n // tn),
        in_specs=[
            pl.BlockSpec((tm, d), lambda i, j: (i, 0)),
            pl.BlockSpec((None, 1, d), lambda i, j: (layer, 0, 0)),
            pl.BlockSpec((None, d, tn), lambda i, j: (layer, 0, j)),
            tab_spec, tab_spec, tab_spec,
        ],
        out_specs=pl.BlockSpec((tm, tn), lambda i, j: (i, j)),
        out_shape=jax.ShapeDtypeStruct((m, n), BF16),
        scratch_shapes=[pltpu.VMEM((tm, d), BF16)],
        compiler_params=_params("arbitrary", "arbitrary"),
        name="in_proj",
    )(x, g, w, cos_t, sa_t, sb_t)


def _attn_kernel(lam_ref, g_ref, qd_ref, kd_ref, vd_ref, qs_ref, ks_ref, vs_ref, od_ref, os_ref,
                 dbias_ref, sbias_ref, qdt_ref, qst_ref, vdt_ref, vst_ref,
                 s_ref, p_ref, al_ref, dacc_ref, z_ref, hi_ref, lo_ref, a_ref,
                 *, t, hps, lambda_init):
    qi = pl.program_id(2)
    n_blk = vdt_ref.shape[1]
    half = t // 2
    acc_rows = HEAD_DIM + BF16_SUBLANES
    heads = range(hps)

    @pl.when((pl.program_id(0) == 0) & (pl.program_id(1) == 0) & (qi == 0))
    def _():
        key = lax.broadcasted_iota(jnp.int32, (t, 2 * t), 0)
        qry = lax.broadcasted_iota(jnp.int32, (t, 2 * t), 1)
        qry = jnp.where(qry >= t, qry - t, qry)
        dbias_ref[0] = jnp.zeros((t, 2 * t), F32)
        dbias_ref[1] = jnp.where(key <= qry, 0.0, MASK_VALUE)
        dbias_ref[2] = jnp.full((t, 2 * t), MASK_VALUE, F32)
        skey = lax.broadcasted_iota(jnp.int32, (t, t), 0)
        sqry = lax.broadcasted_iota(jnp.int32, (t, t), 1)
        sbias_ref[0] = jnp.zeros((t, t), F32)
        sbias_ref[1] = jnp.where(skey < sqry, 0.0, MASK_VALUE)
        sbias_ref[2] = jnp.full((t, t), MASK_VALUE, F32)
        for hh in heads:
            s_ref[hh] = jnp.full((t, 2 * t), MASK_VALUE, F32)
            p_ref[hh] = jnp.zeros((t, 2 * t), BF16)
            al_ref[hh] = jnp.ones((F32_SUBLANES, 2 * t), F32)
            z_ref[hh] = jnp.full((t, t), MASK_VALUE, F32)
            hi_ref[hh] = jnp.zeros((t, t), BF16)
            lo_ref[hh] = jnp.zeros((t, t), BF16)
            a_ref[hh] = jnp.zeros((t, t), BF16)

    @pl.when(qi == 0)
    def _():
        def xpose(c, carry):
            rows = pl.ds(pl.multiple_of(c * t, t), t)
            for hh in heads:
                cols = slice(hh * HEAD_DIM, (hh + 1) * HEAD_DIM)
                vdt_ref[hh, c, :HEAD_DIM, :] = vd_ref[rows, cols].astype(F32).T.astype(BF16)
                vdt_ref[hh, c, HEAD_DIM:, :] = jnp.ones((BF16_SUBLANES, t), BF16)
                vst_ref[hh, c] = vs_ref[rows, cols].astype(F32).T.astype(BF16)
            return carry
        lax.fori_loop(0, n_blk, xpose, 0)

    lane = lax.broadcasted_iota(jnp.int32, (t, HEAD_DIM), 1)
    for hh in heads:
        cols = slice(hh * HEAD_DIM, (hh + 1) * HEAD_DIM)
        q = qd_ref[:, cols].astype(F32) * (DIFF_QK_DIM ** -0.5)
        qq = jnp.concatenate([jnp.where(lane < DIFF_QK_DIM, q, 0.0),
                              jnp.where(lane >= DIFF_QK_DIM, q, 0.0)], axis=0)
        qdt_ref[hh] = qq.T.astype(BF16)
        qst_ref[hh] = qs_ref[:, cols].astype(F32).T.astype(BF16)
        dacc_ref[hh] = jnp.zeros((acc_rows, 2 * t), F32)

    us = lax.broadcasted_iota(jnp.int32, (half, 2 * half), 0)
    uj = lax.broadcasted_iota(jnp.int32, (half, 2 * half), 1)
    uj = jnp.where(uj >= half, uj - half, uj)
    suffix_op = jnp.where(uj >= us, 1.0, 0.0).astype(BF16)
    zscale = (HEAD_DIM ** -0.5) * LOG2E
    sign_bit = jnp.int32(-2 ** 31)

    def body(it, carry):
        blk3 = jnp.clip(qi - it + 2, 0, qi)
        blk1 = jnp.maximum(qi - it, 0)
        sel = jnp.where(it == 0, 1, jnp.where(it <= qi, 0, 2))
        krows = pl.ds(pl.multiple_of(blk1 * t, t), t)
        out = []
        for hh in heads:
            m, c, sacc = carry[hh]
            cols = slice(hh * HEAD_DIM, (hh + 1) * HEAD_DIM)
            dacc_ref[hh] = al_ref[hh][0:1] * dacc_ref[hh] + jnp.dot(vdt_ref[hh, blk3], p_ref[hh],
                                                                     preferred_element_type=F32)
            sacc = sacc + jnp.dot(vst_ref[hh, blk3], a_ref[hh], preferred_element_type=F32)
            s = s_ref[hh]
            m_new = jnp.maximum(m, jnp.max(s, axis=0, keepdims=True))
            al_ref[hh] = jnp.broadcast_to(jnp.exp2((m - m_new) * LOG2E), (F32_SUBLANES, 2 * t))
            p_ref[hh] = jnp.exp2((s - m_new) * LOG2E).astype(BF16)
            z = z_ref[hh]
            hi = hi_ref[hh]
            lo = lo_ref[hh]
            inc_r = jnp.dot(suffix_op, jnp.concatenate([hi[half:], lo[half:]], axis=0),
                            preferred_element_type=F32)
            inc_l = jnp.dot(suffix_op, jnp.concatenate([hi[:half], lo[:half]], axis=0),
                            preferred_element_type=F32)
            c_l = c + inc_r[0:1]
            a_ref[hh, half:, :] = jnp.exp2(z[half:] - inc_r - c).astype(BF16)
            a_ref[hh, :half, :] = jnp.exp2(z[:half] - inc_l - c_l).astype(BF16)
            c = c_l + inc_l[0:1]
            s_ref[hh] = jnp.dot(kd_ref[krows, cols], qdt_ref[hh], preferred_element_type=F32) + dbias_ref[sel]
            zz = jnp.dot(ks_ref[krows, cols], qst_ref[hh], preferred_element_type=F32) * zscale + sbias_ref[sel]
            neg_abs = lax.bitcast_convert_type(lax.bitcast_convert_type(zz, jnp.int32) | sign_bit, F32)
            sp = jnp.maximum(zz, 0.0) + jnp.log2(1.0 + jnp.exp2(neg_abs))
            sp_hi = sp.astype(BF16)
            z_ref[hh] = zz
            hi_ref[hh] = sp_hi
            lo_ref[hh] = (sp - sp_hi.astype(F32)).astype(BF16)
            out.append((m_new, c, sacc))
        return tuple(out)

    init = tuple((jnp.full((1, 2 * t), SOFTMAX_M_INIT, F32), jnp.zeros((1, t), F32),
                  jnp.zeros((HEAD_DIM, t), F32)) for _ in heads)
    final = lax.fori_loop(0, qi + 3, body, init)

    lp = lam_ref[...]
    lam = (jnp.exp(jnp.sum(lp[0:1] * lp[1:2], axis=1, keepdims=True))
           - jnp.exp(jnp.sum(lp[2:3] * lp[3:4], axis=1, keepdims=True)) + lambda_init)
    for hh in heads:
        cols = slice(hh * HEAD_DIM, (hh + 1) * HEAD_DIM)
        dacc = dacc_ref[hh]
        num, den = dacc[:HEAD_DIM], dacc[HEAD_DIM:HEAD_DIM + 1]
        o_t = num[:, :t] / den[:, :t] - lam * (num[:, t:] / den[:, t:])
        o = _rms_scale(o_t.T, SUBLN_EPS) * g_ref[...] * (1.0 - lambda_init)
        od_ref[:, cols] = o.astype(BF16)
        os_ref[:, cols] = final[hh][2].T.astype(BF16)


def _attention(proj, lam_params, subln, layer, lambda_init, batch, seq_len, n_heads):
    t = _tile(seq_len, 256)
    nq = seq_len // t
    hps = ATTN_HEADS_PER_STEP if n_heads % ATTN_HEADS_PER_STEP == 0 else 1
    w = hps * HEAD_DIM
    g = n_heads // hps
    q_spec = lambda grp: pl.BlockSpec((t, w), lambda b, h, i: (b * nq + i, grp * g + h))
    kv_spec = lambda grp: pl.BlockSpec((seq_len, w), lambda b, h, i: (b, grp * g + h))
    o_spec = pl.BlockSpec((t, w), lambda b, h, i: (b * nq + i, h))
    o_shape = jax.ShapeDtypeStruct((batch * seq_len, n_heads * HEAD_DIM), BF16)
    acc_rows = HEAD_DIM + BF16_SUBLANES
    return pl.pallas_call(
        functools.partial(_attn_kernel, t=t, hps=hps, lambda_init=lambda_init),
        grid=(batch, g, nq),
        in_specs=[
            pl.BlockSpec((None, 4, DIFF_QK_DIM), lambda b, h, i: (layer, 0, 0)),
            pl.BlockSpec((None, 1, HEAD_DIM), lambda b, h, i: (layer, 0, 0)),
            q_spec(0), kv_spec(1), kv_spec(2), q_spec(3), kv_spec(4), kv_spec(5),
        ],
        out_specs=[o_spec, o_spec],
        out_shape=[o_shape, o_shape],
        scratch_shapes=[
            pltpu.VMEM((3, t, 2 * t), F32), pltpu.VMEM((3, t, t), F32),
            pltpu.VMEM((hps, HEAD_DIM, 2 * t), BF16), pltpu.VMEM((hps, HEAD_DIM, t), BF16),
            pltpu.VMEM((hps, nq, acc_rows, t), BF16), pltpu.VMEM((hps, nq, HEAD_DIM, t), BF16),
            pltpu.VMEM((hps, t, 2 * t), F32), pltpu.VMEM((hps, t, 2 * t), BF16),
            pltpu.VMEM((hps, F32_SUBLANES, 2 * t), F32), pltpu.VMEM((hps, acc_rows, 2 * t), F32),
            pltpu.VMEM((hps, t, t), F32), pltpu.VMEM((hps, t, t), BF16),
            pltpu.VMEM((hps, t, t), BF16), pltpu.VMEM((hps, t, t), BF16),
        ],
        compiler_params=_params("arbitrary", "arbitrary", "arbitrary"),
        name="attention",
    )(lam_params, subln, proj, proj, proj, proj, proj, proj)


def _out_proj_kernel(ad_ref, as_ref, w_ref, g_ref, x_ref, o_ref):
    dd = ad_ref.shape[1]
    mix = (jnp.dot(ad_ref[...], w_ref[:dd, :], preferred_element_type=F32)
           + jnp.dot(as_ref[...], w_ref[dd:, :], preferred_element_type=F32))
    o_ref[...] = x_ref[...] + _rms_scale(mix, NORM_EPS) * g_ref[...]


def _out_proj(o_d, o_s, w, g, x, layer):
    m, d = x.shape
    dd, ds = o_d.shape[1], o_s.shape[1]
    tm = _tile(m, 512)
    return pl.pallas_call(
        _out_proj_kernel,
        grid=(m // tm,),
        in_specs=[
            pl.BlockSpec((tm, dd), lambda i: (i, 0)),
            pl.BlockSpec((tm, ds), lambda i: (i, 0)),
            pl.BlockSpec((None, dd + ds, d), lambda i: (layer, 0, 0)),
            pl.BlockSpec((None, 1, d), lambda i: (layer, 0, 0)),
            pl.BlockSpec((tm, d), lambda i: (i, 0)),
        ],
        out_specs=pl.BlockSpec((tm, d), lambda i: (i, 0)),
        out_shape=jax.ShapeDtypeStruct((m, d), F32),
        compiler_params=_params("arbitrary"),
        name="out_proj",
    )(o_d, o_s, w, g, x)


def _ffn_kernel(x_ref, halo_ref, gpre_ref, wg_ref, wv_ref, cwg_ref, cwv_ref, cbg_ref, cbv_ref,
                wd_ref, gpost_ref, o_ref, h_ref, acc_ref, *, tiles_per_seq):
    i = pl.program_id(0)
    f = pl.program_id(1)
    tm = x_ref.shape[0]
    halo = halo_ref.shape[0]

    @pl.when(f == 0)
    def _():
        g = gpre_ref[...]
        h_ref[halo:, :] = (_rms_scale(x_ref[...], NORM_EPS) * g).astype(BF16)
        hh = _rms_scale(halo_ref[...], NORM_EPS) * g
        hh = jnp.where(i % tiles_per_seq == 0, 0.0, hh)
        h_ref[:halo, :] = hh.astype(BF16)
        acc_ref[...] = jnp.zeros_like(acc_ref)

    h = h_ref[...]

    def conv(w_ref, cw_ref, cb_ref):
        u = jnp.dot(h, w_ref[...], preferred_element_type=F32)
        cw = cw_ref[...]
        out = cb_ref[...]
        for tap in range(CONV_WIDTH):
            off = halo - (CONV_WIDTH - 1) + tap
            out = out + cw[tap:tap + 1, :] * u[off:off + tm, :]
        return out

    gate = conv(wg_ref, cwg_ref, cbg_ref)
    val = conv(wv_ref, cwv_ref, cbv_ref)
    gelu = 0.5 * gate * (1.0 + jnp.tanh(math.sqrt(2.0 / math.pi) * (gate + 0.044715 * (gate * gate * gate))))
    acc_ref[...] += jnp.dot((gelu * val).astype(BF16), wd_ref[...], preferred_element_type=F32)

    @pl.when(f == pl.num_programs(1) - 1)
    def _():
        o_ref[...] = x_ref[...] + _rms_scale(acc_ref[...], NORM_EPS) * gpost_ref[...]


def _ffn(x, gpre, w_up, conv_w, conv_b, w_down, gpost, layer, seq_len):
    m, d = x.shape
    d_ff = w_down.shape[1]
    tm = _tile(seq_len, 512)
    tf = _tile(d_ff, 512)
    nf = d_ff // tf
    halo = BF16_SUBLANES
    tiles_per_seq = seq_len // tm
    vec = lambda off: pl.BlockSpec((None, 1, tf), lambda i, f: (layer, 0, off + f))
    return pl.pallas_call(
        functools.partial(_ffn_kernel, tiles_per_seq=tiles_per_seq),
        grid=(m // tm, nf),
        in_specs=[
            pl.BlockSpec((tm, d), lambda i, f: (i, 0)),
            pl.BlockSpec((halo, d), lambda i, f: (jnp.maximum(i * (tm // halo) - 1, 0), 0)),
            pl.BlockSpec((None, 1, d), lambda i, f: (layer, 0, 0)),
            pl.BlockSpec((None, d, tf), lambda i, f: (layer, 0, f)),
            pl.BlockSpec((None, d, tf), lambda i, f: (layer, 0, nf + f)),
            pl.BlockSpec((None, CONV_WIDTH, tf), lambda i, f: (layer, 0, f)),
            pl.BlockSpec((None, CONV_WIDTH, tf), lambda i, f: (layer, 0, nf + f)),
            vec(0), vec(nf),
            pl.BlockSpec((None, tf, d), lambda i, f: (layer, f, 0)),
            pl.BlockSpec((None, 1, d), lambda i, f: (layer, 0, 0)),
        ],
        out_specs=pl.BlockSpec((tm, d), lambda i, f: (i, 0)),
        out_shape=jax.ShapeDtypeStruct((m, d), F32),
        scratch_shapes=[pltpu.VMEM((halo + tm, d), BF16), pltpu.VMEM((tm, d), F32)],
        compiler_params=_params("arbitrary", "arbitrary"),
        name="conv_ffn",
    )(x, x, gpre, w_up, w_up, conv_w, conv_w, conv_b, conv_b, w_down, gpost)


def _rope_tables(seq_len):
    inv_freq = ROPE_THETA ** (-jnp.arange(0, ROPE_DIM, 2, dtype=F32) / ROPE_DIM)
    ang = jnp.arange(seq_len, dtype=F32)[:, None] * inv_freq[None, :]
    cos, sin = jnp.cos(ang), jnp.sin(ang)
    rest = DIFF_QK_DIM - ROPE_DIM
    ones = jnp.ones((seq_len, rest), F32)
    z_half = jnp.zeros((seq_len, ROPE_HALF), F32)
    z_rest = jnp.zeros((seq_len, rest), F32)
    reps = HEAD_DIM // DIFF_QK_DIM
    cos_t = jnp.tile(jnp.concatenate([cos, cos, ones], axis=1), (1, reps))
    sa_t = jnp.tile(jnp.concatenate([-sin, z_half, z_rest], axis=1), (1, reps))
    sb_t = jnp.tile(jnp.concatenate([z_half, sin, z_rest], axis=1), (1, reps))
    return cos_t, sa_t, sb_t


def kernel(x, attn_pre_norm, w_in, diff_lambda_q1, diff_lambda_k1, diff_lambda_q2, diff_lambda_k2,
           diff_subln, w_out, attn_post_norm, ffn_pre_norm, ffn_w_up, ffn_conv_w, ffn_conv_b,
           ffn_w_down, ffn_post_norm):
    batch, seq_len, d_model = x.shape
    depth = w_in.shape[0]
    n_heads = d_model // (2 * HEAD_DIM)
    d_group = n_heads * HEAD_DIM
    assert w_in.shape[2] == 6 * d_group

    xf = x.reshape(batch * seq_len, d_model)
    w_in_b, w_out_b = w_in.astype(BF16), w_out.astype(BF16)
    w_up_b, w_down_b = ffn_w_up.astype(BF16), ffn_w_down.astype(BF16)
    row3 = lambda a: a.reshape(depth, 1, a.shape[-1])
    lam_params = jnp.stack([diff_lambda_q1, diff_lambda_k1, diff_lambda_q2, diff_lambda_k2], axis=1)
    tables = _rope_tables(seq_len)

    for l in range(depth):
        lambda_init = 0.8 - 0.6 * float(np.exp(-0.3 * l))
        proj = _in_proj(xf, row3(attn_pre_norm), w_in_b, l, tables, seq_len, 2 * d_group)
        o_d, o_s = _attention(proj, lam_params, row3(diff_subln), l, lambda_init, batch, seq_len, n_heads)
        xf = _out_proj(o_d, o_s, w_out_b, row3(attn_post_norm), xf, l)
        xf = _ffn(xf, row3(ffn_pre_norm), w_up_b, ffn_conv_w, row3(ffn_conv_b), w_down_b,
                  row3(ffn_post_norm), l, seq_len)
    return xf.reshape(batch, seq_len, d_model)
```

```python
import functools
import math

import jax
import jax.numpy as jnp
import numpy as np
from jax import lax
from jax.experimental import pallas as pl
from jax.experimental.pallas import tpu as pltpu

HEAD_DIM = 128
DIFF_QK_DIM = HEAD_DIM // 2
ROPE_DIM = DIFF_QK_DIM // 4
ROPE_HALF = ROPE_DIM // 2
ROPE_THETA = 500000.0
NORM_EPS = 1e-6
SUBLN_EPS = 1e-5
CONV_WIDTH = 3
LANES = 128
F32_SUBLANES = 8
BF16_SUBLANES = 16
ATTN_HEADS_PER_STEP = 2
MASK_VALUE = -1e30
SOFTMAX_M_INIT = -1e29
LOG2E = math.log2(math.e)
VMEM_LIMIT_BYTES = 56 * 1024 * 1024

F32 = jnp.float32
BF16 = jnp.bfloat16


def _tile(n, pref):
    if n <= pref:
        return n
    for t in range(pref - pref % LANES, 0, -LANES):
        if n % t == 0:
            return t
    raise ValueError((n, pref))


def _rms_scale(x, eps):
    return x * lax.rsqrt(jnp.mean(x * x, axis=-1, keepdims=True) + eps)


def _params(*sem):
    return pltpu.CompilerParams(dimension_semantics=sem, vmem_limit_bytes=VMEM_LIMIT_BYTES)


def _in_proj_kernel(x_ref, g_ref, w_ref, cos_ref, sa_ref, sb_ref, o_ref, h_ref, *, n_rope_tiles):
    j = pl.program_id(1)

    @pl.when(j == 0)
    def _():
        h_ref[...] = (_rms_scale(x_ref[...], NORM_EPS) * g_ref[...]).astype(BF16)

    acc = jnp.dot(h_ref[...], w_ref[...], preferred_element_type=F32)
    tn = acc.shape[1]

    @pl.when(j < n_rope_tiles)
    def _():
        for c in range(tn // LANES):
            a = acc[:, c * LANES:(c + 1) * LANES]
            r = (a * cos_ref[...]
                 + pltpu.roll(a, LANES - ROPE_HALF, 1) * sa_ref[...]
                 + pltpu.roll(a, ROPE_HALF, 1) * sb_ref[...])
            o_ref[:, c * LANES:(c + 1) * LANES] = r.astype(BF16)

    @pl.when(j >= n_rope_tiles)
    def _():
        o_ref[...] = acc.astype(BF16)


def _in_proj(x, g, w, layer, tables, seq_len, n_rope_cols):
    m, d = x.shape
    n = w.shape[2]
    tm = _tile(seq_len, 1024)
    tn = _tile(n_rope_cols, 512)
    assert n % tn == 0
    tiles_per_seq = seq_len // tm
    cos_t, sa_t, sb_t = tables
    tab_spec = pl.BlockSpec((tm, LANES), lambda i, j: (i % tiles_per_seq, 0))
    return pl.pallas_call(
        functools.partial(_in_proj_kernel, n_rope_tiles=n_rope_cols // tn),
        grid=(m // tm, n // tn),
        in_specs=[
            pl.BlockSpec((tm, d), lambda i, j: (i, 0)),
            pl.BlockSpec((None, 1, d), lambda i, j: (layer, 0, 0)),
            pl.BlockSpec((None, d, tn), lambda i, j: (layer, 0, j)),
            tab_spec, tab_spec, tab_spec,
        ],
        out_specs=pl.BlockSpec((tm, tn), lambda i, j: (i, j)),
        out_shape=jax.ShapeDtypeStruct((m, n), BF16),
        scratch_shapes=[pltpu.VMEM((tm, d), BF16)],
        compiler_params=_params("arbitrary", "arbitrary"),
        name="in_proj",
    )(x, g, w, cos_t, sa_t, sb_t)


def _attn_kernel(lam_ref, g_ref, qd_ref, kd_ref, vd_ref, qs_ref, ks_ref, vs_ref, od_ref, os_ref,
                 dbias_ref, sbias_ref, qdt_ref, qst_ref, vdt_ref, vst_ref,
                 s_ref, p_ref, al_ref, dacc_ref, z_ref, hi_ref, lo_ref, a_ref, sacc_ref,
                 *, t, hps, lambda_init):
    qi = pl.program_id(2)
    n_blk = vdt_ref.shape[1]
    half = t // 2
    acc_rows = HEAD_DIM + BF16_SUBLANES
    heads = range(hps)

    def clear_value_operands(hh):
        p_ref[hh] = jnp.zeros((t, 2 * t), BF16)
        al_ref[hh] = jnp.ones((F32_SUBLANES, 2 * t), F32)
        a_ref[hh] = jnp.zeros((t, t), BF16)

    @pl.when((pl.program_id(0) == 0) & (pl.program_id(1) == 0) & (qi == 0))
    def _():
        key = lax.broadcasted_iota(jnp.int32, (t, 2 * t), 0)
        qry = lax.broadcasted_iota(jnp.int32, (t, 2 * t), 1)
        qry = jnp.where(qry >= t, qry - t, qry)
        dbias_ref[...] = jnp.where(key <= qry, 0.0, MASK_VALUE)
        skey = lax.broadcasted_iota(jnp.int32, (t, t), 0)
        sqry = lax.broadcasted_iota(jnp.int32, (t, t), 1)
        sbias_ref[...] = jnp.where(skey < sqry, 0.0, MASK_VALUE)

    @pl.when(qi == 0)
    def _():
        def xpose(c, carry):
            rows = pl.ds(pl.multiple_of(c * t, t), t)
            for hh in heads:
                cols = slice(hh * HEAD_DIM, (hh + 1) * HEAD_DIM)
                vdt_ref[hh, c, :HEAD_DIM, :] = vd_ref[rows, cols].astype(F32).T.astype(BF16)
                vdt_ref[hh, c, HEAD_DIM:, :] = jnp.ones((BF16_SUBLANES, t), BF16)
                vst_ref[hh, c] = vs_ref[rows, cols].astype(F32).T.astype(BF16)
            return carry
        lax.fori_loop(0, n_blk, xpose, 0)

    lane = lax.broadcasted_iota(jnp.int32, (t, HEAD_DIM), 1)
    for hh in heads:
        cols = slice(hh * HEAD_DIM, (hh + 1) * HEAD_DIM)
        q = qd_ref[:, cols].astype(F32) * (DIFF_QK_DIM ** -0.5)
        qq = jnp.concatenate([jnp.where(lane < DIFF_QK_DIM, q, 0.0),
                              jnp.where(lane >= DIFF_QK_DIM, q, 0.0)], axis=0)
        qdt_ref[hh] = qq.T.astype(BF16)
        qst_ref[hh] = qs_ref[:, cols].astype(F32).T.astype(BF16)
        dacc_ref[hh] = jnp.zeros((acc_rows, 2 * t), F32)
        sacc_ref[hh] = jnp.zeros((HEAD_DIM, t), F32)
        clear_value_operands(hh)

    us = lax.broadcasted_iota(jnp.int32, (half, 2 * half), 0)
    uj = lax.broadcasted_iota(jnp.int32, (half, 2 * half), 1)
    uj = jnp.where(uj >= half, uj - half, uj)
    suffix_op = jnp.where(uj >= us, 1.0, 0.0).astype(BF16)
    zscale = (HEAD_DIM ** -0.5) * LOG2E
    sign_bit = jnp.int32(-2 ** 31)

    def stages(it, carry, *, value, mid, score, diagonal=False):
        blk3 = jnp.clip(qi - it + 2, 0, qi)
        krows = pl.ds(pl.multiple_of((qi - it) * t, t), t) if score else None
        out = []
        for hh in heads:
            m, c = carry[hh]
            cols = slice(hh * HEAD_DIM, (hh + 1) * HEAD_DIM)
            if value:
                dacc_ref[hh] = al_ref[hh][0:1] * dacc_ref[hh] + jnp.dot(vdt_ref[hh, blk3], p_ref[hh],
                                                                         preferred_element_type=F32)
                sacc_ref[hh] += jnp.dot(vst_ref[hh, blk3], a_ref[hh], preferred_element_type=F32)
            if mid:
                s = s_ref[hh]
                m_new = jnp.maximum(m, jnp.max(s, axis=0, keepdims=True))
                al_ref[hh] = jnp.broadcast_to(jnp.exp2((m - m_new) * LOG2E), (F32_SUBLANES, 2 * t))
                p_ref[hh] = jnp.exp2((s - m_new) * LOG2E).astype(BF16)
                m = m_new
                z = z_ref[hh]
                hi = hi_ref[hh]
                lo = lo_ref[hh]
                inc_r = jnp.dot(suffix_op, jnp.concatenate([hi[half:], lo[half:]], axis=0),
                                preferred_element_type=F32)
                inc_l = jnp.dot(suffix_op, jnp.concatenate([hi[:half], lo[:half]], axis=0),
                                preferred_element_type=F32)
                c_l = c + inc_r[0:1]
                a_ref[hh, half:, :] = jnp.exp2(z[half:] - inc_r - c).astype(BF16)
                a_ref[hh, :half, :] = jnp.exp2(z[:half] - inc_l - c_l).astype(BF16)
                c = c_l + inc_l[0:1]
            if score:
                sd = jnp.dot(kd_ref[krows, cols], qdt_ref[hh], preferred_element_type=F32)
                zz = jnp.dot(ks_ref[krows, cols], qst_ref[hh], preferred_element_type=F32) * zscale
                if diagonal:
                    sd = sd + dbias_ref[...]
                    zz = zz + sbias_ref[...]
                s_ref[hh] = sd
                neg_abs = lax.bitcast_convert_type(lax.bitcast_convert_type(zz, jnp.int32) | sign_bit, F32)
                sp = jnp.maximum(zz, 0.0) + jnp.log2(1.0 + jnp.exp2(neg_abs))
                sp_hi = sp.astype(BF16)
                z_ref[hh] = zz
                hi_ref[hh] = sp_hi
                lo_ref[hh] = (sp - sp_hi.astype(F32)).astype(BF16)
            out.append((m, c))
        return tuple(out)

    carry = tuple((jnp.full((1, 2 * t), SOFTMAX_M_INIT, F32), jnp.zeros((1, t), F32)) for _ in heads)
    carry = stages(0, carry, value=False, mid=False, score=True, diagonal=True)
    carry = lax.fori_loop(1, qi + 1, functools.partial(stages, value=True, mid=True, score=True), carry)
    carry = stages(qi + 1, carry, value=True, mid=True, score=False)
    stages(qi + 2, carry, value=True, mid=False, score=False)

    lp = lam_ref[...]
    lam = (jnp.exp(jnp.sum(lp[0:1] * lp[1:2], axis=1, keepdims=True))
           - jnp.exp(jnp.sum(lp[2:3] * lp[3:4], axis=1, keepdims=True)) + lambda_init)
    for hh in heads:
        cols = slice(hh * HEAD_DIM, (hh + 1) * HEAD_DIM)
        dacc = dacc_ref[hh]
        num, den = dacc[:HEAD_DIM], dacc[HEAD_DIM:HEAD_DIM + 1]
        o_t = num[:, :t] / den[:, :t] - lam * (num[:, t:] / den[:, t:])
        o = _rms_scale(o_t.T, SUBLN_EPS) * g_ref[...] * (1.0 - lambda_init)
        od_ref[:, cols] = o.astype(BF16)
        os_ref[:, cols] = sacc_ref[hh].T.astype(BF16)


def _attention(proj, lam_params, subln, layer, lambda_init, batch, seq_len, n_heads):
    t = _tile(seq_len, 256)
    nq = seq_len // t
    hps = ATTN_HEADS_PER_STEP if n_heads % ATTN_HEADS_PER_STEP == 0 else 1
    w = hps * HEAD_DIM
    g = n_heads // hps
    q_spec = lambda grp: pl.BlockSpec((t, w), lambda b, h, i: (b * nq + i, grp * g + h))
    kv_spec = lambda grp: pl.BlockSpec((seq_len, w), lambda b, h, i: (b, grp * g + h))
    o_spec = pl.BlockSpec((t, w), lambda b, h, i: (b * nq + i, h))
    o_shape = jax.ShapeDtypeStruct((batch * seq_len, n_heads * HEAD_DIM), BF16)
    acc_rows = HEAD_DIM + BF16_SUBLANES
    return pl.pallas_call(
        functools.partial(_attn_kernel, t=t, hps=hps, lambda_init=lambda_init),
        grid=(batch, g, nq),
        in_specs=[
            pl.BlockSpec((None, 4, DIFF_QK_DIM), lambda b, h, i: (layer, 0, 0)),
            pl.BlockSpec((None, 1, HEAD_DIM), lambda b, h, i: (layer, 0, 0)),
            q_spec(0), kv_spec(1), kv_spec(2), q_spec(3), kv_spec(4), kv_spec(5),
        ],
        out_specs=[o_spec, o_spec],
        out_shape=[o_shape, o_shape],
        scratch_shapes=[
            pltpu.VMEM((t, 2 * t), F32), pltpu.VMEM((t, t), F32),
            pltpu.VMEM((hps, HEAD_DIM, 2 * t), BF16), pltpu.VMEM((hps, HEAD_DIM, t), BF16),
            pltpu.VMEM((hps, nq, acc_rows, t), BF16), pltpu.VMEM((hps, nq, HEAD_DIM, t), BF16),
            pltpu.VMEM((hps, t, 2 * t), F32), pltpu.VMEM((hps, t, 2 * t), BF16),
            pltpu.VMEM((hps, F32_SUBLANES, 2 * t), F32), pltpu.VMEM((hps, acc_rows, 2 * t), F32),
            pltpu.VMEM((hps, t, t), F32), pltpu.VMEM((hps, t, t), BF16),
            pltpu.VMEM((hps, t, t), BF16), pltpu.VMEM((hps, t, t), BF16),
            pltpu.VMEM((hps, HEAD_DIM, t), F32),
        ],
        compiler_params=_params("arbitrary", "arbitrary", "arbitrary"),
        name="attention",
    )(lam_params, subln, proj, proj, proj, proj, proj, proj)


def _out_proj_kernel(ad_ref, as_ref, w_ref, g_ref, x_ref, o_ref):
    dd = ad_ref.shape[1]
    mix = (jnp.dot(ad_ref[...], w_ref[:dd, :], preferred_element_type=F32)
           + jnp.dot(as_ref[...], w_ref[dd:, :], preferred_element_type=F32))
    o_ref[...] = x_ref[...] + _rms_scale(mix, NORM_EPS) * g_ref[...]


def _out_proj(o_d, o_s, w, g, x, layer):
    m, d = x.shape
    dd, ds = o_d.shape[1], o_s.shape[1]
    tm = _tile(m, 512)
    return pl.pallas_call(
        _out_proj_kernel,
        grid=(m // tm,),
        in_specs=[
            pl.BlockSpec((tm, dd), lambda i: (i, 0)),
            pl.BlockSpec((tm, ds), lambda i: (i, 0)),
            pl.BlockSpec((None, dd + ds, d), lambda i: (layer, 0, 0)),
            pl.BlockSpec((None, 1, d), lambda i: (layer, 0, 0)),
            pl.BlockSpec((tm, d), lambda i: (i, 0)),
        ],
        out_specs=pl.BlockSpec((tm, d), lambda i: (i, 0)),
        out_shape=jax.ShapeDtypeStruct((m, d), F32),
        compiler_params=_params("arbitrary"),
        name="out_proj",
    )(o_d, o_s, w, g, x)


def _ffn_kernel(x_ref, halo_ref, gpre_ref, wg_ref, wv_ref, cwg_ref, cwv_ref, cbg_ref, cbv_ref,
                wd_ref, gpost_ref, o_ref, h_ref, acc_ref, *, tiles_per_seq):
    i = pl.program_id(0)
    f = pl.program_id(1)
    tm = x_ref.shape[0]
    halo = halo_ref.shape[0]

    @pl.when(f == 0)
    def _():
        g = gpre_ref[...]
        h_ref[halo:, :] = (_rms_scale(x_ref[...], NORM_EPS) * g).astype(BF16)
        hh = _rms_scale(halo_ref[...], NORM_EPS) * g
        hh = jnp.where(i % tiles_per_seq == 0, 0.0, hh)
        h_ref[:halo, :] = hh.astype(BF16)
        acc_ref[...] = jnp.zeros_like(acc_ref)

    h = h_ref[...]

    def conv(w_ref, cw_ref, cb_ref):
        u = jnp.dot(h, w_ref[...], preferred_element_type=F32)
        cw = cw_ref[...]
        out = cb_ref[...]
        for tap in range(CONV_WIDTH):
            off = halo - (CONV_WIDTH - 1) + tap
            out = out + cw[tap:tap + 1, :] * u[off:off + tm, :]
        return out

    gate = conv(wg_ref, cwg_ref, cbg_ref)
    val = conv(wv_ref, cwv_ref, cbv_ref)
    gelu = 0.5 * gate * (1.0 + jnp.tanh(math.sqrt(2.0 / math.pi) * (gate + 0.044715 * (gate * gate * gate))))
    acc_ref[...] += jnp.dot((gelu * val).astype(BF16), wd_ref[...], preferred_element_type=F32)

    @pl.when(f == pl.num_programs(1) - 1)
    def _():
        o_ref[...] = x_ref[...] + _rms_scale(acc_ref[...], NORM_EPS) * gpost_ref[...]


def _ffn(x, gpre, w_up, conv_w, conv_b, w_down, gpost, layer, seq_len):
    m, d = x.shape
    d_ff = w_down.shape[1]
    tm = _tile(seq_len, 512)
    tf = _tile(d_ff, 512)
    nf = d_ff // tf
    halo = BF16_SUBLANES
    tiles_per_seq = seq_len // tm
    vec = lambda off: pl.BlockSpec((None, 1, tf), lambda i, f: (layer, 0, off + f))
    return pl.pallas_call(
        functools.partial(_ffn_kernel, tiles_per_seq=tiles_per_seq),
        grid=(m // tm, nf),
        in_specs=[
            pl.BlockSpec((tm, d), lambda i, f: (i, 0)),
            pl.BlockSpec((halo, d), lambda i, f: (jnp.maximum(i * (tm // halo) - 1, 0), 0)),
            pl.BlockSpec((None, 1, d), lambda i, f: (layer, 0, 0)),
            pl.BlockSpec((None, d, tf), lambda i, f: (layer, 0, f)),
            pl.BlockSpec((None, d, tf), lambda i, f: (layer, 0, nf + f)),
            pl.BlockSpec((None, CONV_WIDTH, tf), lambda i, f: (layer, 0, f)),
            pl.BlockSpec((None, CONV_WIDTH, tf), lambda i, f: (layer, 0, nf + f)),
            vec(0), vec(nf),
            pl.BlockSpec((None, tf, d), lambda i, f: (layer, f, 0)),
            pl.BlockSpec((None, 1, d), lambda i, f: (layer, 0, 0)),
        ],
        out_specs=pl.BlockSpec((tm, d), lambda i, f: (i, 0)),
        out_shape=jax.ShapeDtypeStruct((m, d), F32),
        scratch_shapes=[pltpu.VMEM((halo + tm, d), BF16), pltpu.VMEM((tm, d), F32)],
        compiler_params=_params("arbitrary", "arbitrary"),
        name="conv_ffn",
    )(x, x, gpre, w_up, w_up, conv_w, conv_w, conv_b, conv_b, w_down, gpost)


def _rope_tables(seq_len):
    inv_freq = ROPE_THETA ** (-jnp.arange(0, ROPE_DIM, 2, dtype=F32) / ROPE_DIM)
    ang = jnp.arange(seq_len, dtype=F32)[:, None] * inv_freq[None, :]
    cos, sin = jnp.cos(ang), jnp.sin(ang)
    rest = DIFF_QK_DIM - ROPE_DIM
    ones = jnp.ones((seq_len, rest), F32)
    z_half = jnp.zeros((seq_len, ROPE_HALF), F32)
    z_rest = jnp.zeros((seq_len, rest), F32)
    reps = HEAD_DIM // DIFF_QK_DIM
    cos_t = jnp.tile(jnp.concatenate([cos, cos, ones], axis=1), (1, reps))
    sa_t = jnp.tile(jnp.concatenate([-sin, z_half, z_rest], axis=1), (1, reps))
    sb_t = jnp.tile(jnp.concatenate([z_half, sin, z_rest], axis=1), (1, reps))
    return cos_t, sa_t, sb_t


def kernel(x, attn_pre_norm, w_in, diff_lambda_q1, diff_lambda_k1, diff_lambda_q2, diff_lambda_k2,
           diff_subln, w_out, attn_post_norm, ffn_pre_norm, ffn_w_up, ffn_conv_w, ffn_conv_b,
           ffn_w_down, ffn_post_norm):
    batch, seq_len, d_model = x.shape
    depth = w_in.shape[0]
    n_heads = d_model // (2 * HEAD_DIM)
    d_group = n_heads * HEAD_DIM
    assert w_in.shape[2] == 6 * d_group

    xf = x.reshape(batch * seq_len, d_model)
    w_in_b, w_out_b = w_in.astype(BF16), w_out.astype(BF16)
    w_up_b, w_down_b = ffn_w_up.astype(BF16), ffn_w_down.astype(BF16)
    row3 = lambda a: a.reshape(depth, 1, a.shape[-1])
    lam_params = jnp.stack([diff_lambda_q1, diff_lambda_k1, diff_lambda_q2, diff_lambda_k2], axis=1)
    tables = _rope_tables(seq_len)

    for l in range(depth):
        lambda_init = 0.8 - 0.6 * float(np.exp(-0.3 * l))
        proj = _in_proj(xf, row3(attn_pre_norm), w_in_b, l, tables, seq_len, 2 * d_group)
        o_d, o_s = _attention(proj, lam_params, row3(diff_subln), l, lambda_init, batch, seq_len, n_heads)
        xf = _out_proj(o_d, o_s, w_out_b, row3(attn_post_norm), xf, l)
        xf = _ffn(xf, row3(ffn_pre_norm), w_up_b, ffn_conv_w, row3(ffn_conv_b), w_down_b,
                  row3(ffn_post_norm), l, seq_len)
    return xf.reshape(batch, seq_len, d_model)
```

```python
import functools
import math

import jax
import jax.numpy as jnp
import numpy as np
from jax import lax
from jax.experimental import pallas as pl
from jax.experimental.pallas import tpu as pltpu

HEAD_DIM = 128
DIFF_QK_DIM = HEAD_DIM // 2
ROPE_DIM = DIFF_QK_DIM // 4
ROPE_HALF = ROPE_DIM // 2
ROPE_THETA = 500000.0
NORM_EPS = 1e-6
SUBLN_EPS = 1e-5
CONV_WIDTH = 3
LANES = 128
F32_SUBLANES = 8
BF16_SUBLANES = 16
ATTN_HEADS_PER_STEP = 4
MASK_VALUE = -1e30
SOFTMAX_M_INIT = -1e29
LOG2E = math.log2(math.e)
VMEM_LIMIT_BYTES = 56 * 1024 * 1024

F32 = jnp.float32
BF16 = jnp.bfloat16


def _tile(n, pref):
    if n <= pref:
        return n
    for t in range(pref - pref % LANES, 0, -LANES):
        if n % t == 0:
            return t
    raise ValueError((n, pref))


def _rms_scale(x, eps):
    return x * lax.rsqrt(jnp.mean(x * x, axis=-1, keepdims=True) + eps)


def _params(*sem):
    return pltpu.CompilerParams(dimension_semantics=sem, vmem_limit_bytes=VMEM_LIMIT_BYTES)


def _in_proj_kernel(x_ref, g_ref, w_ref, cos_ref, sa_ref, sb_ref, o_ref, h_ref, *, n_rope_tiles):
    j = pl.program_id(1)

    @pl.when(j == 0)
    def _():
        h_ref[...] = (_rms_scale(x_ref[...], NORM_EPS) * g_ref[...]).astype(BF16)

    acc = jnp.dot(h_ref[...], w_ref[...], preferred_element_type=F32)
    tn = acc.shape[1]

    @pl.when(j < n_rope_tiles)
    def _():
        for c in range(tn // LANES):
            a = acc[:, c * LANES:(c + 1) * LANES]
            r = (a * cos_ref[...]
                 + pltpu.roll(a, LANES - ROPE_HALF, 1) * sa_ref[...]
                 + pltpu.roll(a, ROPE_HALF, 1) * sb_ref[...])
            o_ref[:, c * LANES:(c + 1) * LANES] = r.astype(BF16)

    @pl.when(j >= n_rope_tiles)
    def _():
        o_ref[...] = acc.astype(BF16)


def _in_proj(x, g, w, layer, tables, seq_len, n_rope_cols):
    m, d = x.shape
    n = w.shape[2]
    tm = _tile(seq_len, 1024)
    tn = _tile(n_rope_cols, 512)
    assert n % tn == 0
    tiles_per_seq = seq_len // tm
    cos_t, sa_t, sb_t = tables
    tab_spec = pl.BlockSpec((tm, LANES), lambda i, j: (i % tiles_per_seq, 0))
    return pl.pallas_call(
        functools.partial(_in_proj_kernel, n_rope_tiles=n_rope_cols // tn),
        grid=(m // tm, n // tn),
        in_specs=[
            pl.BlockSpec((tm, d), lambda i, j: (i, 0)),
            pl.BlockSpec((None, 1, d), lambda i, j: (layer, 0, 0)),
            pl.BlockSpec((None, d, tn), lambda i, j: (layer, 0, j)),
            tab_spec, tab_spec, tab_spec,
        ],
        out_specs=pl.BlockSpec((tm, tn), lambda i, j: (i, j)),
        out_shape=jax.ShapeDtypeStruct((m, n), BF16),
        scratch_shapes=[pltpu.VMEM((tm, d), BF16)],
        compiler_params=_params("arbitrary", "arbitrary"),
        name="in_proj",
    )(x, g, w, cos_t, sa_t, sb_t)


def _attn_kernel(lam_ref, g_ref, qd_ref, kd_ref, vdt_ref, qs_ref, ks_ref, vst_ref, od_ref, os_ref,
                 dbias_ref, sbias_ref, qdt_ref, qst_ref,
                 s_ref, p_ref, al_ref, dacc_ref, z_ref, hi_ref, lo_ref, a_ref, sacc_ref,
                 *, t, hps, lambda_init):
    qi = pl.program_id(2)
    half = t // 2
    acc_rows = HEAD_DIM + BF16_SUBLANES
    heads = range(hps)

    def clear_value_operands(hh):
        p_ref[hh] = jnp.zeros((t, 2 * t), BF16)
        al_ref[hh] = jnp.ones((F32_SUBLANES, 2 * t), F32)
        a_ref[hh] = jnp.zeros((t, t), BF16)

    @pl.when((pl.program_id(0) == 0) & (pl.program_id(1) == 0) & (qi == 0))
    def _():
        key = lax.broadcasted_iota(jnp.int32, (t, 2 * t), 0)
        qry = lax.broadcasted_iota(jnp.int32, (t, 2 * t), 1)
        qry = jnp.where(qry >= t, qry - t, qry)
        dbias_ref[...] = jnp.where(key <= qry, 0.0, MASK_VALUE)
        skey = lax.broadcasted_iota(jnp.int32, (t, t), 0)
        sqry = lax.broadcasted_iota(jnp.int32, (t, t), 1)
        sbias_ref[...] = jnp.where(skey < sqry, 0.0, MASK_VALUE)

    lane = lax.broadcasted_iota(jnp.int32, (t, HEAD_DIM), 1)
    for hh in heads:
        cols = slice(hh * HEAD_DIM, (hh + 1) * HEAD_DIM)
        q = qd_ref[:, cols].astype(F32) * (DIFF_QK_DIM ** -0.5)
        qq = jnp.concatenate([jnp.where(lane < DIFF_QK_DIM, q, 0.0),
                              jnp.where(lane >= DIFF_QK_DIM, q, 0.0)], axis=0)
        qdt_ref[hh] = qq.T.astype(BF16)
        qst_ref[hh] = qs_ref[:, cols].astype(F32).T.astype(BF16)
        dacc_ref[hh] = jnp.zeros((acc_rows, 2 * t), F32)
        sacc_ref[hh] = jnp.zeros((HEAD_DIM, t), F32)
        clear_value_operands(hh)

    us = lax.broadcasted_iota(jnp.int32, (half, 2 * half), 0)
    uj = lax.broadcasted_iota(jnp.int32, (half, 2 * half), 1)
    uj = jnp.where(uj >= half, uj - half, uj)
    suffix_op = jnp.where(uj >= us, 1.0, 0.0).astype(BF16)
    zscale = (HEAD_DIM ** -0.5) * LOG2E
    sign_bit = jnp.int32(-2 ** 31)

    def stages(it, carry, *, value, mid, score, diagonal=False):
        blk3 = jnp.clip(qi - it + 2, 0, qi)
        krows = pl.ds(pl.multiple_of((qi - it) * t, t), t) if score else None
        out = []
        for hh in heads:
            m, c = carry[hh]
            cols = slice(hh * HEAD_DIM, (hh + 1) * HEAD_DIM)
            if value:
                dacc_ref[hh] = al_ref[hh][0:1] * dacc_ref[hh] + jnp.dot(vdt_ref[hh, blk3], p_ref[hh],
                                                                         preferred_element_type=F32)
                sacc_ref[hh] += jnp.dot(vst_ref[hh, blk3], a_ref[hh], preferred_element_type=F32)
            if mid:
                s = s_ref[hh]
                m_new = jnp.maximum(m, jnp.max(s, axis=0, keepdims=True))
                al_ref[hh] = jnp.broadcast_to(jnp.exp2((m - m_new) * LOG2E), (F32_SUBLANES, 2 * t))
                p_ref[hh] = jnp.exp2((s - m_new) * LOG2E).astype(BF16)
                m = m_new
                z = z_ref[hh]
                hi = hi_ref[hh]
                lo = lo_ref[hh]
                inc_r = jnp.dot(suffix_op, jnp.concatenate([hi[half:], lo[half:]], axis=0),
                                preferred_element_type=F32)
                inc_l = jnp.dot(suffix_op, jnp.concatenate([hi[:half], lo[:half]], axis=0),
                                preferred_element_type=F32)
                c_l = c + inc_r[0:1]
                a_ref[hh, half:, :] = jnp.exp2(z[half:] - inc_r - c).astype(BF16)
                a_ref[hh, :half, :] = jnp.exp2(z[:half] - inc_l - c_l).astype(BF16)
                c = c_l + inc_l[0:1]
            if score:
                sd = jnp.dot(kd_ref[krows, cols], qdt_ref[hh], preferred_element_type=F32)
                zz = jnp.dot(ks_ref[krows, cols], qst_ref[hh], preferred_element_type=F32) * zscale
                if diagonal:
                    sd = sd + dbias_ref[...]
                    zz = zz + sbias_ref[...]
                s_ref[hh] = sd
                neg_abs = lax.bitcast_convert_type(lax.bitcast_convert_type(zz, jnp.int32) | sign_bit, F32)
                sp = jnp.maximum(zz, 0.0) + jnp.log2(1.0 + jnp.exp2(neg_abs))
                sp_hi = sp.astype(BF16)
                z_ref[hh] = zz
                hi_ref[hh] = sp_hi
                lo_ref[hh] = (sp - sp_hi.astype(F32)).astype(BF16)
            out.append((m, c))
        return tuple(out)

    carry = tuple((jnp.full((1, 2 * t), SOFTMAX_M_INIT, F32), jnp.zeros((1, t), F32)) for _ in heads)
    carry = stages(0, carry, value=False, mid=False, score=True, diagonal=True)
    carry = lax.fori_loop(1, qi + 1, functools.partial(stages, value=True, mid=True, score=True), carry)
    carry = stages(qi + 1, carry, value=True, mid=True, score=False)
    stages(qi + 2, carry, value=True, mid=False, score=False)

    lp = lam_ref[...]
    lam = (jnp.exp(jnp.sum(lp[0:1] * lp[1:2], axis=1, keepdims=True))
           - jnp.exp(jnp.sum(lp[2:3] * lp[3:4], axis=1, keepdims=True)) + lambda_init)
    for hh in heads:
        cols = slice(hh * HEAD_DIM, (hh + 1) * HEAD_DIM)
        dacc = dacc_ref[hh]
        num, den = dacc[:HEAD_DIM], dacc[HEAD_DIM:HEAD_DIM + 1]
        o_t = num[:, :t] / den[:, :t] - lam * (num[:, t:] / den[:, t:])
        o = _rms_scale(o_t.T, SUBLN_EPS) * g_ref[...] * (1.0 - lambda_init)
        od_ref[:, cols] = o.astype(BF16)
        os_ref[:, cols] = sacc_ref[hh].T.astype(BF16)


def _vt_kernel(vd_ref, vs_ref, vdt_ref, vst_ref, *, t):
    def xpose(c, carry):
        rows = pl.ds(pl.multiple_of(c * t, t), t)
        vdt_ref[c, :HEAD_DIM, :] = vd_ref[rows, :].astype(F32).T.astype(BF16)
        vdt_ref[c, HEAD_DIM:, :] = jnp.ones((BF16_SUBLANES, t), BF16)
        vst_ref[c] = vs_ref[rows, :].astype(F32).T.astype(BF16)
        return carry
    lax.fori_loop(0, vdt_ref.shape[0], xpose, 0)


def _value_transposes(proj, batch, seq_len, n_heads, t):
    nq = seq_len // t
    acc_rows = HEAD_DIM + BF16_SUBLANES
    v_spec = lambda grp: pl.BlockSpec((seq_len, HEAD_DIM), lambda b, h: (b, grp * n_heads + h))
    out_spec = lambda rows: pl.BlockSpec((None, None, nq, rows, t), lambda b, h: (b, h, 0, 0, 0))
    out_shape = lambda rows: jax.ShapeDtypeStruct((batch, n_heads, nq, rows, t), BF16)
    return pl.pallas_call(
        functools.partial(_vt_kernel, t=t),
        grid=(batch, n_heads),
        in_specs=[v_spec(2), v_spec(5)],
        out_specs=[out_spec(acc_rows), out_spec(HEAD_DIM)],
        out_shape=[out_shape(acc_rows), out_shape(HEAD_DIM)],
        compiler_params=_params("arbitrary", "arbitrary"),
        name="value_transposes",
    )(proj, proj)


def _attention(proj, lam_params, subln, layer, lambda_init, batch, seq_len, n_heads):
    t = _tile(seq_len, 256)
    nq = seq_len // t
    hps = ATTN_HEADS_PER_STEP if n_heads % ATTN_HEADS_PER_STEP == 0 else 1
    w = hps * HEAD_DIM
    g = n_heads // hps
    acc_rows = HEAD_DIM + BF16_SUBLANES
    vdt, vst = _value_transposes(proj, batch, seq_len, n_heads, t)
    once = pl.Buffered(1)
    q_spec = lambda grp: pl.BlockSpec((t, w), lambda b, h, i: (b * nq + i, grp * g + h))
    k_spec = lambda grp: pl.BlockSpec((seq_len, w), lambda b, h, i: (b, grp * g + h), pipeline_mode=once)
    vt_spec = lambda rows: pl.BlockSpec((None, hps, nq, rows, t), lambda b, h, i: (b, h, 0, 0, 0),
                                        pipeline_mode=once)
    o_spec = pl.BlockSpec((t, w), lambda b, h, i: (b * nq + i, h))
    o_shape = jax.ShapeDtypeStruct((batch * seq_len, n_heads * HEAD_DIM), BF16)
    return pl.pallas_call(
        functools.partial(_attn_kernel, t=t, hps=hps, lambda_init=lambda_init),
        grid=(batch, g, nq),
        in_specs=[
            pl.BlockSpec((None, 4, DIFF_QK_DIM), lambda b, h, i: (layer, 0, 0)),
            pl.BlockSpec((None, 1, HEAD_DIM), lambda b, h, i: (layer, 0, 0)),
            q_spec(0), k_spec(1), vt_spec(acc_rows), q_spec(3), k_spec(4), vt_spec(HEAD_DIM),
        ],
        out_specs=[o_spec, o_spec],
        out_shape=[o_shape, o_shape],
        scratch_shapes=[
            pltpu.VMEM((t, 2 * t), F32), pltpu.VMEM((t, t), F32),
            pltpu.VMEM((hps, HEAD_DIM, 2 * t), BF16), pltpu.VMEM((hps, HEAD_DIM, t), BF16),
            pltpu.VMEM((hps, t, 2 * t), F32), pltpu.VMEM((hps, t, 2 * t), BF16),
            pltpu.VMEM((hps, F32_SUBLANES, 2 * t), F32), pltpu.VMEM((hps, acc_rows, 2 * t), F32),
            pltpu.VMEM((hps, t, t), F32), pltpu.VMEM((hps, t, t), BF16),
            pltpu.VMEM((hps, t, t), BF16), pltpu.VMEM((hps, t, t), BF16),
            pltpu.VMEM((hps, HEAD_DIM, t), F32),
        ],
        compiler_params=_params("arbitrary", "arbitrary", "arbitrary"),
        name="attention",
    )(lam_params, subln, proj, proj, vdt, proj, proj, vst)


def _out_proj_kernel(ad_ref, as_ref, w_ref, g_ref, x_ref, o_ref):
    dd = ad_ref.shape[1]
    mix = (jnp.dot(ad_ref[...], w_ref[:dd, :], preferred_element_type=F32)
           + jnp.dot(as_ref[...], w_ref[dd:, :], preferred_element_type=F32))
    o_ref[...] = x_ref[...] + _rms_scale(mix, NORM_EPS) * g_ref[...]


def _out_proj(o_d, o_s, w, g, x, layer):
    m, d = x.shape
    dd, ds = o_d.shape[1], o_s.shape[1]
    tm = _tile(m, 512)
    return pl.pallas_call(
        _out_proj_kernel,
        grid=(m // tm,),
        in_specs=[
            pl.BlockSpec((tm, dd), lambda i: (i, 0)),
            pl.BlockSpec((tm, ds), lambda i: (i, 0)),
            pl.BlockSpec((None, dd + ds, d), lambda i: (layer, 0, 0)),
            pl.BlockSpec((None, 1, d), lambda i: (layer, 0, 0)),
            pl.BlockSpec((tm, d), lambda i: (i, 0)),
        ],
        out_specs=pl.BlockSpec((tm, d), lambda i: (i, 0)),
        out_shape=jax.ShapeDtypeStruct((m, d), F32),
        compiler_params=_params("arbitrary"),
        name="out_proj",
    )(o_d, o_s, w, g, x)


def _ffn_kernel(x_ref, halo_ref, gpre_ref, wg_ref, wv_ref, cwg_ref, cwv_ref, cbg_ref, cbv_ref,
                wd_ref, gpost_ref, o_ref, h_ref, acc_ref, *, tiles_per_seq):
    i = pl.program_id(0)
    f = pl.program_id(1)
    tm = x_ref.shape[0]
    halo = halo_ref.shape[0]

    @pl.when(f == 0)
    def _():
        g = gpre_ref[...]
        h_ref[halo:, :] = (_rms_scale(x_ref[...], NORM_EPS) * g).astype(BF16)
        hh = _rms_scale(halo_ref[...], NORM_EPS) * g
        hh = jnp.where(i % tiles_per_seq == 0, 0.0, hh)
        h_ref[:halo, :] = hh.astype(BF16)
        acc_ref[...] = jnp.zeros_like(acc_ref)

    h = h_ref[...]

    def conv(w_ref, cw_ref, cb_ref):
        u = jnp.dot(h, w_ref[...], preferred_element_type=F32)
        cw = cw_ref[...]
        out = cb_ref[...]
        for tap in range(CONV_WIDTH):
            off = halo - (CONV_WIDTH - 1) + tap
            out = out + cw[tap:tap + 1, :] * u[off:off + tm, :]
        return out

    gate = conv(wg_ref, cwg_ref, cbg_ref)
    val = conv(wv_ref, cwv_ref, cbv_ref)
    gelu = 0.5 * gate * (1.0 + jnp.tanh(math.sqrt(2.0 / math.pi) * (gate + 0.044715 * (gate * gate * gate))))
    acc_ref[...] += jnp.dot((gelu * val).astype(BF16), wd_ref[...], preferred_element_type=F32)

    @pl.when(f == pl.num_programs(1) - 1)
    def _():
        o_ref[...] = x_ref[...] + _rms_scale(acc_ref[...], NORM_EPS) * gpost_ref[...]


def _ffn(x, gpre, w_up, conv_w, conv_b, w_down, gpost, layer, seq_len):
    m, d = x.shape
    d_ff = w_down.shape[1]
    tm = _tile(seq_len, 512)
    tf = _tile(d_ff, 512)
    nf = d_ff // tf
    halo = BF16_SUBLANES
    tiles_per_seq = seq_len // tm
    vec = lambda off: pl.BlockSpec((None, 1, tf), lambda i, f: (layer, 0, off + f))
    return pl.pallas_call(
        functools.partial(_ffn_kernel, tiles_per_seq=tiles_per_seq),
        grid=(m // tm, nf),
        in_specs=[
            pl.BlockSpec((tm, d), lambda i, f: (i, 0)),
            pl.BlockSpec((halo, d), lambda i, f: (jnp.maximum(i * (tm // halo) - 1, 0), 0)),
            pl.BlockSpec((None, 1, d), lambda i, f: (layer, 0, 0)),
            pl.BlockSpec((None, d, tf), lambda i, f: (layer, 0, f)),
            pl.BlockSpec((None, d, tf), lambda i, f: (layer, 0, nf + f)),
            pl.BlockSpec((None, CONV_WIDTH, tf), lambda i, f: (layer, 0, f)),
            pl.BlockSpec((None, CONV_WIDTH, tf), lambda i, f: (layer, 0, nf + f)),
            vec(0), vec(nf),
            pl.BlockSpec((None, tf, d), lambda i, f: (layer, f, 0)),
            pl.BlockSpec((None, 1, d), lambda i, f: (layer, 0, 0)),
        ],
        out_specs=pl.BlockSpec((tm, d), lambda i, f: (i, 0)),
        out_shape=jax.ShapeDtypeStruct((m, d), F32),
        scratch_shapes=[pltpu.VMEM((halo + tm, d), BF16), pltpu.VMEM((tm, d), F32)],
        compiler_params=_params("arbitrary", "arbitrary"),
        name="conv_ffn",
    )(x, x, gpre, w_up, w_up, conv_w, conv_w, conv_b, conv_b, w_down, gpost)


def _rope_tables(seq_len):
    inv_freq = ROPE_THETA ** (-jnp.arange(0, ROPE_DIM, 2, dtype=F32) / ROPE_DIM)
    ang = jnp.arange(seq_len, dtype=F32)[:, None] * inv_freq[None, :]
    cos, sin = jnp.cos(ang), jnp.sin(ang)
    rest = DIFF_QK_DIM - ROPE_DIM
    ones = jnp.ones((seq_len, rest), F32)
    z_half = jnp.zeros((seq_len, ROPE_HALF), F32)
    z_rest = jnp.zeros((seq_len, rest), F32)
    reps = HEAD_DIM // DIFF_QK_DIM
    cos_t = jnp.tile(jnp.concatenate([cos, cos, ones], axis=1), (1, reps))
    sa_t = jnp.tile(jnp.concatenate([-sin, z_half, z_rest], axis=1), (1, reps))
    sb_t = jnp.tile(jnp.concatenate([z_half, sin, z_rest], axis=1), (1, reps))
    return cos_t, sa_t, sb_t


def kernel(x, attn_pre_norm, w_in, diff_lambda_q1, diff_lambda_k1, diff_lambda_q2, diff_lambda_k2,
           diff_subln, w_out, attn_post_norm, ffn_pre_norm, ffn_w_up, ffn_conv_w, ffn_conv_b,
           ffn_w_down, ffn_post_norm):
    batch, seq_len, d_model = x.shape
    depth = w_in.shape[0]
    n_heads = d_model // (2 * HEAD_DIM)
    d_group = n_heads * HEAD_DIM
    assert w_in.shape[2] == 6 * d_group

    xf = x.reshape(batch * seq_len, d_model)
    w_in_b, w_out_b = w_in.astype(BF16), w_out.astype(BF16)
    w_up_b, w_down_b = ffn_w_up.astype(BF16), ffn_w_down.astype(BF16)
    row3 = lambda a: a.reshape(depth, 1, a.shape[-1])
    lam_params = jnp.stack([diff_lambda_q1, diff_lambda_k1, diff_lambda_q2, diff_lambda_k2], axis=1)
    tables = _rope_tables(seq_len)

    for l in range(depth):
        lambda_init = 0.8 - 0.6 * float(np.exp(-0.3 * l))
        proj = _in_proj(xf, row3(attn_pre_norm), w_in_b, l, tables, seq_len, 2 * d_group)
        o_d, o_s = _attention(proj, lam_params, row3(diff_subln), l, lambda_init, batch, seq_len, n_heads)
        xf = _out_proj(o_d, o_s, w_out_b, row3(attn_post_norm), xf, l)
        xf = _ffn(xf, row3(ffn_pre_norm), w_up_b, ffn_conv_w, row3(ffn_conv_b), w_down_b,
                  row3(ffn_post_norm), l, seq_len)
    return xf.reshape(batch, seq_len, d_model)
```

```python
import functools
import math

import jax
import jax.numpy as jnp
import numpy as np
from jax import lax
from jax.experimental import pallas as pl
from jax.experimental.pallas import tpu as pltpu

HEAD_DIM = 128
DIFF_QK_DIM = HEAD_DIM // 2
ROPE_DIM = DIFF_QK_DIM // 4
ROPE_HALF = ROPE_DIM // 2
ROPE_THETA = 500000.0
NORM_EPS = 1e-6
SUBLN_EPS = 1e-5
CONV_WIDTH = 3
LANES = 128
F32_SUBLANES = 8
BF16_SUBLANES = 16
ATTN_HEADS_PER_STEP = 4
MASK_VALUE = -1e30
SOFTMAX_M_INIT = -1e29
LOG2E = math.log2(math.e)
VMEM_LIMIT_BYTES = 56 * 1024 * 1024

F32 = jnp.float32
BF16 = jnp.bfloat16


def _tile(n, pref):
    if n <= pref:
        return n
    for t in range(pref - pref % LANES, 0, -LANES):
        if n % t == 0:
            return t
    raise ValueError((n, pref))


def _rms_scale(x, eps):
    return x * lax.rsqrt(jnp.mean(x * x, axis=-1, keepdims=True) + eps)


def _params(*sem):
    return pltpu.CompilerParams(dimension_semantics=sem, vmem_limit_bytes=VMEM_LIMIT_BYTES)


def _in_proj_kernel(x_ref, g_ref, w_ref, cos_ref, sa_ref, sb_ref, o_ref, h_ref, *, n_rope_tiles):
    j = pl.program_id(1)

    @pl.when(j == 0)
    def _():
        h_ref[...] = (_rms_scale(x_ref[...], NORM_EPS) * g_ref[...]).astype(BF16)

    acc = jnp.dot(h_ref[...], w_ref[...], preferred_element_type=F32)
    tn = acc.shape[1]

    @pl.when(j < n_rope_tiles)
    def _():
        for c in range(tn // LANES):
            a = acc[:, c * LANES:(c + 1) * LANES]
            r = (a * cos_ref[...]
                 + pltpu.roll(a, LANES - ROPE_HALF, 1) * sa_ref[...]
                 + pltpu.roll(a, ROPE_HALF, 1) * sb_ref[...])
            o_ref[:, c * LANES:(c + 1) * LANES] = r.astype(BF16)

    @pl.when(j >= n_rope_tiles)
    def _():
        o_ref[...] = acc.astype(BF16)


def _in_proj(x, g, w, layer, tables, seq_len, n_rope_cols):
    m, d = x.shape
    n = w.shape[2]
    tm = _tile(seq_len, 1024)
    tn = _tile(n_rope_cols, 512)
    assert n % tn == 0
    tiles_per_seq = seq_len // tm
    cos_t, sa_t, sb_t = tables
    tab_spec = pl.BlockSpec((tm, LANES), lambda i, j: (i % tiles_per_seq, 0))
    return pl.pallas_call(
        functools.partial(_in_proj_kernel, n_rope_tiles=n_rope_cols // tn),
        grid=(m // tm, n // tn),
        in_specs=[
            pl.BlockSpec((tm, d), lambda i, j: (i, 0)),
            pl.BlockSpec((None, 1, d), lambda i, j: (layer, 0, 0)),
            pl.BlockSpec((None, d, tn), lambda i, j: (layer, 0, j)),
            tab_spec, tab_spec, tab_spec,
        ],
        out_specs=pl.BlockSpec((tm, tn), lambda i, j: (i, j)),
        out_shape=jax.ShapeDtypeStruct((m, n), BF16),
        scratch_shapes=[pltpu.VMEM((tm, d), BF16)],
        compiler_params=_params("arbitrary", "arbitrary"),
        name="in_proj",
    )(x, g, w, cos_t, sa_t, sb_t)


def _attn_kernel(lam_ref, g_ref, qd_ref, kd_ref, vdt_ref, qs_ref, ks_ref, vst_ref, od_ref, os_ref,
                 dbias_ref, sbias_ref, qdt_ref, qst_ref,
                 s_ref, p_ref, al_ref, dacc_ref, z_ref, hi_ref, lo_ref, a_ref, sacc_ref,
                 *, t, hps, lambda_init):
    qi = pl.program_id(2)
    half = t // 2
    acc_rows = HEAD_DIM + BF16_SUBLANES
    heads = range(hps)

    def clear_value_operands(hh):
        p_ref[hh] = jnp.zeros((t, 2 * t), BF16)
        al_ref[hh] = jnp.ones((F32_SUBLANES, 2 * t), F32)
        a_ref[hh] = jnp.zeros((t, t), BF16)

    @pl.when((pl.program_id(0) == 0) & (pl.program_id(1) == 0) & (qi == 0))
    def _():
        key = lax.broadcasted_iota(jnp.int32, (t, 2 * t), 0)
        qry = lax.broadcasted_iota(jnp.int32, (t, 2 * t), 1)
        qry = jnp.where(qry >= t, qry - t, qry)
        dbias_ref[...] = jnp.where(key <= qry, 0.0, MASK_VALUE)
        skey = lax.broadcasted_iota(jnp.int32, (t, t), 0)
        sqry = lax.broadcasted_iota(jnp.int32, (t, t), 1)
        sbias_ref[...] = jnp.where(skey < sqry, 0.0, MASK_VALUE)

    lane = lax.broadcasted_iota(jnp.int32, (t, HEAD_DIM), 1)
    for hh in heads:
        cols = slice(hh * HEAD_DIM, (hh + 1) * HEAD_DIM)
        q = qd_ref[:, cols].astype(F32) * ((DIFF_QK_DIM ** -0.5) * LOG2E)
        qq = jnp.concatenate([jnp.where(lane < DIFF_QK_DIM, q, 0.0),
                              jnp.where(lane >= DIFF_QK_DIM, q, 0.0)], axis=0)
        qdt_ref[hh] = qq.T.astype(BF16)
        qst_ref[hh] = (qs_ref[:, cols].astype(F32) * ((HEAD_DIM ** -0.5) * LOG2E)).T.astype(BF16)
        dacc_ref[hh] = jnp.zeros((acc_rows, 2 * t), F32)
        sacc_ref[hh] = jnp.zeros((HEAD_DIM, t), F32)
        clear_value_operands(hh)

    us = lax.broadcasted_iota(jnp.int32, (half, 2 * half), 0)
    uj = lax.broadcasted_iota(jnp.int32, (half, 2 * half), 1)
    uj = jnp.where(uj >= half, uj - half, uj)
    suffix_op = jnp.where(uj >= us, 1.0, 0.0).astype(BF16)
    sign_bit = jnp.int32(-2 ** 31)

    def stages(it, carry, *, value, mid, score, diagonal=False):
        blk3 = jnp.clip(qi - it + 2, 0, qi)
        krows = pl.ds(pl.multiple_of((qi - it) * t, t), t) if score else None
        out = []
        for hh in heads:
            m, c = carry[hh]
            cols = slice(hh * HEAD_DIM, (hh + 1) * HEAD_DIM)
            if value:
                dacc_ref[hh] = al_ref[hh][0:1] * dacc_ref[hh] + jnp.dot(vdt_ref[hh, blk3], p_ref[hh],
                                                                         preferred_element_type=F32)
                sacc_ref[hh] += jnp.dot(vst_ref[hh, blk3], a_ref[hh], preferred_element_type=F32)
            if mid:
                s = s_ref[hh]
                m_new = jnp.maximum(m, jnp.max(s, axis=0, keepdims=True))
                al_ref[hh] = jnp.broadcast_to(jnp.exp2(m - m_new), (F32_SUBLANES, 2 * t))
                p_ref[hh] = jnp.exp2(s - m_new).astype(BF16)
                m = m_new
                z = z_ref[hh]
                hi = hi_ref[hh]
                lo = lo_ref[hh]
                inc_r = jnp.dot(suffix_op, jnp.concatenate([hi[half:], lo[half:]], axis=0),
                                preferred_element_type=F32)
                inc_l = jnp.dot(suffix_op, jnp.concatenate([hi[:half], lo[:half]], axis=0),
                                preferred_element_type=F32)
                c_l = c + inc_r[0:1]
                a_ref[hh, half:, :] = jnp.exp2(z[half:] - inc_r - c).astype(BF16)
                a_ref[hh, :half, :] = jnp.exp2(z[:half] - inc_l - c_l).astype(BF16)
                c = c_l + inc_l[0:1]
            if score:
                sd = jnp.dot(kd_ref[krows, cols], qdt_ref[hh], preferred_element_type=F32)
                zz = jnp.dot(ks_ref[krows, cols], qst_ref[hh], preferred_element_type=F32)
                if diagonal:
                    sd = sd + dbias_ref[...]
                    zz = zz + sbias_ref[...]
                s_ref[hh] = sd
                neg_abs = lax.bitcast_convert_type(lax.bitcast_convert_type(zz, jnp.int32) | sign_bit, F32)
                sp = jnp.maximum(zz, 0.0) + jnp.log2(1.0 + jnp.exp2(neg_abs))
                sp_hi = sp.astype(BF16)
                z_ref[hh] = zz
                hi_ref[hh] = sp_hi
                lo_ref[hh] = (sp - sp_hi.astype(F32)).astype(BF16)
            out.append((m, c))
        return tuple(out)

    carry = tuple((jnp.full((1, 2 * t), SOFTMAX_M_INIT, F32), jnp.zeros((1, t), F32)) for _ in heads)
    carry = stages(0, carry, value=False, mid=False, score=True, diagonal=True)
    carry = lax.fori_loop(1, qi + 1, functools.partial(stages, value=True, mid=True, score=True), carry)
    carry = stages(qi + 1, carry, value=True, mid=True, score=False)
    stages(qi + 2, carry, value=True, mid=False, score=False)

    lp = lam_ref[...]
    lam = (jnp.exp(jnp.sum(lp[0:1] * lp[1:2], axis=1, keepdims=True))
           - jnp.exp(jnp.sum(lp[2:3] * lp[3:4], axis=1, keepdims=True)) + lambda_init)
    for hh in heads:
        cols = slice(hh * HEAD_DIM, (hh + 1) * HEAD_DIM)
        dacc = dacc_ref[hh]
        num, den = dacc[:HEAD_DIM], dacc[HEAD_DIM:HEAD_DIM + 1]
        o_t = num[:, :t] / den[:, :t] - lam * (num[:, t:] / den[:, t:])
        o = _rms_scale(o_t.T, SUBLN_EPS) * g_ref[...] * (1.0 - lambda_init)
        od_ref[:, cols] = o.astype(BF16)
        os_ref[:, cols] = sacc_ref[hh].T.astype(BF16)


def _vt_kernel(vd_ref, vs_ref, vdt_ref, vst_ref, *, t):
    def xpose(c, carry):
        rows = pl.ds(pl.multiple_of(c * t, t), t)
        vdt_ref[c, :HEAD_DIM, :] = vd_ref[rows, :].astype(F32).T.astype(BF16)
        vdt_ref[c, HEAD_DIM:, :] = jnp.ones((BF16_SUBLANES, t), BF16)
        vst_ref[c] = vs_ref[rows, :].astype(F32).T.astype(BF16)
        return carry
    lax.fori_loop(0, vdt_ref.shape[0], xpose, 0)


def _value_transposes(proj, batch, seq_len, n_heads, t):
    nq = seq_len // t
    acc_rows = HEAD_DIM + BF16_SUBLANES
    v_spec = lambda grp: pl.BlockSpec((seq_len, HEAD_DIM), lambda b, h: (b, grp * n_heads + h))
    out_spec = lambda rows: pl.BlockSpec((None, None, nq, rows, t), lambda b, h: (b, h, 0, 0, 0))
    out_shape = lambda rows: jax.ShapeDtypeStruct((batch, n_heads, nq, rows, t), BF16)
    return pl.pallas_call(
        functools.partial(_vt_kernel, t=t),
        grid=(batch, n_heads),
        in_specs=[v_spec(2), v_spec(5)],
        out_specs=[out_spec(acc_rows), out_spec(HEAD_DIM)],
        out_shape=[out_shape(acc_rows), out_shape(HEAD_DIM)],
        compiler_params=_params("arbitrary", "arbitrary"),
        name="value_transposes",
    )(proj, proj)


def _attention(proj, lam_params, subln, layer, lambda_init, batch, seq_len, n_heads):
    t = _tile(seq_len, 256)
    nq = seq_len // t
    hps = ATTN_HEADS_PER_STEP if n_heads % ATTN_HEADS_PER_STEP == 0 else 1
    w = hps * HEAD_DIM
    g = n_heads // hps
    acc_rows = HEAD_DIM + BF16_SUBLANES
    vdt, vst = _value_transposes(proj, batch, seq_len, n_heads, t)
    once = pl.Buffered(1)
    q_spec = lambda grp: pl.BlockSpec((t, w), lambda b, h, i: (b * nq + i, grp * g + h))
    k_spec = lambda grp: pl.BlockSpec((seq_len, w), lambda b, h, i: (b, grp * g + h), pipeline_mode=once)
    vt_spec = lambda rows: pl.BlockSpec((None, hps, nq, rows, t), lambda b, h, i: (b, h, 0, 0, 0),
                                        pipeline_mode=once)
    o_spec = pl.BlockSpec((t, w), lambda b, h, i: (b * nq + i, h))
    o_shape = jax.ShapeDtypeStruct((batch * seq_len, n_heads * HEAD_DIM), BF16)
    return pl.pallas_call(
        functools.partial(_attn_kernel, t=t, hps=hps, lambda_init=lambda_init),
        grid=(batch, g, nq),
        in_specs=[
            pl.BlockSpec((None, 4, DIFF_QK_DIM), lambda b, h, i: (layer, 0, 0)),
            pl.BlockSpec((None, 1, HEAD_DIM), lambda b, h, i: (layer, 0, 0)),
            q_spec(0), k_spec(1), vt_spec(acc_rows), q_spec(3), k_spec(4), vt_spec(HEAD_DIM),
        ],
        out_specs=[o_spec, o_spec],
        out_shape=[o_shape, o_shape],
        scratch_shapes=[
            pltpu.VMEM((t, 2 * t), F32), pltpu.VMEM((t, t), F32),
            pltpu.VMEM((hps, HEAD_DIM, 2 * t), BF16), pltpu.VMEM((hps, HEAD_DIM, t), BF16),
            pltpu.VMEM((hps, t, 2 * t), F32), pltpu.VMEM((hps, t, 2 * t), BF16),
            pltpu.VMEM((hps, F32_SUBLANES, 2 * t), F32), pltpu.VMEM((hps, acc_rows, 2 * t), F32),
            pltpu.VMEM((hps, t, t), F32), pltpu.VMEM((hps, t, t), BF16),
            pltpu.VMEM((hps, t, t), BF16), pltpu.VMEM((hps, t, t), BF16),
            pltpu.VMEM((hps, HEAD_DIM, t), F32),
        ],
        compiler_params=_params("arbitrary", "arbitrary", "arbitrary"),
        name="attention",
    )(lam_params, subln, proj, proj, vdt, proj, proj, vst)


def _out_proj_kernel(ad_ref, as_ref, w_ref, g_ref, x_ref, o_ref):
    dd = ad_ref.shape[1]
    mix = (jnp.dot(ad_ref[...], w_ref[:dd, :], preferred_element_type=F32)
           + jnp.dot(as_ref[...], w_ref[dd:, :], preferred_element_type=F32))
    o_ref[...] = x_ref[...] + _rms_scale(mix, NORM_EPS) * g_ref[...]


def _out_proj(o_d, o_s, w, g, x, layer):
    m, d = x.shape
    dd, ds = o_d.shape[1], o_s.shape[1]
    tm = _tile(m, 512)
    return pl.pallas_call(
        _out_proj_kernel,
        grid=(m // tm,),
        in_specs=[
            pl.BlockSpec((tm, dd), lambda i: (i, 0)),
            pl.BlockSpec((tm, ds), lambda i: (i, 0)),
            pl.BlockSpec((None, dd + ds, d), lambda i: (layer, 0, 0)),
            pl.BlockSpec((None, 1, d), lambda i: (layer, 0, 0)),
            pl.BlockSpec((tm, d), lambda i: (i, 0)),
        ],
        out_specs=pl.BlockSpec((tm, d), lambda i: (i, 0)),
        out_shape=jax.ShapeDtypeStruct((m, d), F32),
        compiler_params=_params("arbitrary"),
        name="out_proj",
    )(o_d, o_s, w, g, x)


def _ffn_kernel(x_ref, halo_ref, gpre_ref, wg_ref, wv_ref, cwg_ref, cwv_ref, cbg_ref, cbv_ref,
                wd_ref, gpost_ref, o_ref, h_ref, acc_ref, *, tiles_per_seq):
    i = pl.program_id(0)
    f = pl.program_id(1)
    tm = x_ref.shape[0]
    halo = halo_ref.shape[0]

    @pl.when(f == 0)
    def _():
        g = gpre_ref[...]
        h_ref[halo:, :] = (_rms_scale(x_ref[...], NORM_EPS) * g).astype(BF16)
        hh = _rms_scale(halo_ref[...], NORM_EPS) * g
        hh = jnp.where(i % tiles_per_seq == 0, 0.0, hh)
        h_ref[:halo, :] = hh.astype(BF16)
        acc_ref[...] = jnp.zeros_like(acc_ref)

    h = h_ref[...]

    def conv(w_ref, cw_ref, cb_ref):
        u = jnp.dot(h, w_ref[...], preferred_element_type=F32)
        cw = cw_ref[...]
        out = cb_ref[...]
        for tap in range(CONV_WIDTH):
            off = halo - (CONV_WIDTH - 1) + tap
            out = out + cw[tap:tap + 1, :] * u[off:off + tm, :]
        return out

    gate = conv(wg_ref, cwg_ref, cbg_ref)
    val = conv(wv_ref, cwv_ref, cbv_ref)
    gelu = 0.5 * gate * (1.0 + jnp.tanh(math.sqrt(2.0 / math.pi) * (gate + 0.044715 * (gate * gate * gate))))
    acc_ref[...] += jnp.dot((gelu * val).astype(BF16), wd_ref[...], preferred_element_type=F32)

    @pl.when(f == pl.num_programs(1) - 1)
    def _():
        o_ref[...] = x_ref[...] + _rms_scale(acc_ref[...], NORM_EPS) * gpost_ref[...]


def _ffn(x, gpre, w_up, conv_w, conv_b, w_down, gpost, layer, seq_len):
    m, d = x.shape
    d_ff = w_down.shape[1]
    tm = _tile(seq_len, 512)
    tf = _tile(d_ff, 512)
    nf = d_ff // tf
    halo = BF16_SUBLANES
    tiles_per_seq = seq_len // tm
    vec = lambda off: pl.BlockSpec((None, 1, tf), lambda i, f: (layer, 0, off + f))
    return pl.pallas_call(
        functools.partial(_ffn_kernel, tiles_per_seq=tiles_per_seq),
        grid=(m // tm, nf),
        in_specs=[
            pl.BlockSpec((tm, d), lambda i, f: (i, 0)),
            pl.BlockSpec((halo, d), lambda i, f: (jnp.maximum(i * (tm // halo) - 1, 0), 0)),
            pl.BlockSpec((None, 1, d), lambda i, f: (layer, 0, 0)),
            pl.BlockSpec((None, d, tf), lambda i, f: (layer, 0, f)),
            pl.BlockSpec((None, d, tf), lambda i, f: (layer, 0, nf + f)),
            pl.BlockSpec((None, CONV_WIDTH, tf), lambda i, f: (layer, 0, f)),
            pl.BlockSpec((None, CONV_WIDTH, tf), lambda i, f: (layer, 0, nf + f)),
            vec(0), vec(nf),
            pl.BlockSpec((None, tf, d), lambda i, f: (layer, f, 0)),
            pl.BlockSpec((None, 1, d), lambda i, f: (layer, 0, 0)),
        ],
        out_specs=pl.BlockSpec((tm, d), lambda i, f: (i, 0)),
        out_shape=jax.ShapeDtypeStruct((m, d), F32),
        scratch_shapes=[pltpu.VMEM((halo + tm, d), BF16), pltpu.VMEM((tm, d), F32)],
        compiler_params=_params("arbitrary", "arbitrary"),
        name="conv_ffn",
    )(x, x, gpre, w_up, w_up, conv_w, conv_w, conv_b, conv_b, w_down, gpost)


def _rope_tables(seq_len):
    inv_freq = ROPE_THETA ** (-jnp.arange(0, ROPE_DIM, 2, dtype=F32) / ROPE_DIM)
    ang = jnp.arange(seq_len, dtype=F32)[:, None] * inv_freq[None, :]
    cos, sin = jnp.cos(ang), jnp.sin(ang)
    rest = DIFF_QK_DIM - ROPE_DIM
    ones = jnp.ones((seq_len, rest), F32)
    z_half = jnp.zeros((seq_len, ROPE_HALF), F32)
    z_rest = jnp.zeros((seq_len, rest), F32)
    reps = HEAD_DIM // DIFF_QK_DIM
    cos_t = jnp.tile(jnp.concatenate([cos, cos, ones], axis=1), (1, reps))
    sa_t = jnp.tile(jnp.concatenate([-sin, z_half, z_rest], axis=1), (1, reps))
    sb_t = jnp.tile(jnp.concatenate([z_half, sin, z_rest], axis=1), (1, reps))
    return cos_t, sa_t, sb_t


def kernel(x, attn_pre_norm, w_in, diff_lambda_q1, diff_lambda_k1, diff_lambda_q2, diff_lambda_k2,
           diff_subln, w_out, attn_post_norm, ffn_pre_norm, ffn_w_up, ffn_conv_w, ffn_conv_b,
           ffn_w_down, ffn_post_norm):
    batch, seq_len, d_model = x.shape
    depth = w_in.shape[0]
    n_heads = d_model // (2 * HEAD_DIM)
    d_group = n_heads * HEAD_DIM
    assert w_in.shape[2] == 6 * d_group

    xf = x.reshape(batch * seq_len, d_model)
    w_in_b, w_out_b = w_in.astype(BF16), w_out.astype(BF16)
    w_up_b, w_down_b = ffn_w_up.astype(BF16), ffn_w_down.astype(BF16)
    row3 = lambda a: a.reshape(depth, 1, a.shape[-1])
    lam_params = jnp.stack([diff_lambda_q1, diff_lambda_k1, diff_lambda_q2, diff_lambda_k2], axis=1)
    tables = _rope_tables(seq_len)

    for l in range(depth):
        lambda_init = 0.8 - 0.6 * float(np.exp(-0.3 * l))
        proj = _in_proj(xf, row3(attn_pre_norm), w_in_b, l, tables, seq_len, 2 * d_group)
        o_d, o_s = _attention(proj, lam_params, row3(diff_subln), l, lambda_init, batch, seq_len, n_heads)
        xf = _out_proj(o_d, o_s, w_out_b, row3(attn_post_norm), xf, l)
        xf = _ffn(xf, row3(ffn_pre_norm), w_up_b, ffn_conv_w, row3(ffn_conv_b), w_down_b,
                  row3(ffn_post_norm), l, seq_len)
    return xf.reshape(batch, seq_len, d_model)
```
